```python
import jax
import jax.numpy as jnp
from jax import lax
import numpy as np

D_MODEL = 4096
BATCH = 2
SEQ = 4096
DEPTH = 4

GRID_W = 64
CTX_LEN = 256
N_MIXERS = 2
EPS = 1e-6

ML_HEADS = 8
ML_QK = D_MODEL // 2
ML_V = D_MODEL
ML_DQK = ML_QK // ML_HEADS
ML_DV = ML_V // ML_HEADS
ML_CHUNK = 64
GATE_CAP = 15.0
ML_IN = 2 * ML_QK + 2 * ML_V + 4 * ML_HEADS

MLA_HEADS = D_MODEL // 128
MLA_NOPE = 128
MLA_ROPE = 64
MLA_V = 128
Q_LORA = D_MODEL // 4
KV_LORA = D_MODEL // 8
MLA_DOWN = Q_LORA + KV_LORA + MLA_ROPE
MLA_SCALE = (MLA_NOPE + MLA_ROPE) ** -0.5
Q_BLOCK = 128
ROPE_THETA = 10000.0

D_FF = 3 * D_MODEL // 2

N_MLSTM = (DEPTH + 1) // 2
N_MLA = DEPTH // 2

kernel_name = 'hybrid_mlstm_mla_convffn_dit'


def rms_norm(x, g):
    xf = x.astype(jnp.float32)
    y = xf * lax.rsqrt(jnp.mean(xf * xf, axis=-1, keepdims=True) + EPS)
    return (y * g.astype(jnp.float32)).astype(x.dtype)


def adaln(cvec, w, b, n):
    m = jax.nn.silu(cvec) @ w[:, :n * D_MODEL] + b[:n * D_MODEL]
    return jnp.split(m[..., None, :], n, axis=-1)


def modulate(x, g, shift, scale):
    return rms_norm(x, g) * (1.0 + scale) + shift


def axial_rope_tables(n_lat):
    rows = n_lat // GRID_W
    row, col = jnp.meshgrid(jnp.arange(rows), jnp.arange(GRID_W), indexing='ij')
    per_axis = MLA_ROPE // 2
    inv_freq = ROPE_THETA ** (-jnp.arange(0, per_axis, 2, dtype=jnp.float32) / per_axis)
    ang_r = row.reshape(-1, 1, 1).astype(jnp.float32) * inv_freq
    ang_c = col.reshape(-1, 1, 1).astype(jnp.float32) * inv_freq
    return (jnp.cos(ang_r), jnp.sin(ang_r), jnp.cos(ang_c), jnp.sin(ang_c))


def _rotate(x, cos, sin):
    x1, x2 = jnp.split(x, 2, axis=-1)
    return jnp.concatenate([x1 * cos - x2 * sin, x1 * sin + x2 * cos], axis=-1)


def apply_axial_rope(x, rope):
    cos_r, sin_r, cos_c, sin_c = rope
    xr, xc = jnp.split(x.astype(jnp.float32), 2, axis=-1)
    return jnp.concatenate([_rotate(xr, cos_r, sin_r), _rotate(xc, cos_c, sin_c)], axis=-1).astype(x.dtype)


def mlstm_scan(q, k, v, i_pre, logf, state):
    B, H, T, _ = q.shape
    dv = v.shape[-1]
    L = ML_CHUNK
    nc = T // L

    def chunks(a):
        return jnp.moveaxis(a.reshape(B, H, nc, L, *a.shape[3:]), 2, 0)

    causal_in_chunk = jnp.tril(jnp.ones((L, L), dtype=bool))

    def step(carry, xs):
        C, n, m = carry
        qc, kc, vc, ic, fc = xs
        b = jnp.cumsum(fc, axis=-1)
        dmat = b[..., :, None] - b[..., None, :] + ic[..., None, :]
        dmat = jnp.where(causal_in_chunk, dmat, -jnp.inf)
        inter = b + m[..., None]
        m_out = jnp.maximum(inter, jnp.max(dmat, axis=-1))
        w_intra = jnp.exp(dmat - m_out[..., None])
        w_inter = jnp.exp(inter - m_out)
        s = jnp.einsum('bhtd,bhsd->bhts', qc, kc) * w_intra
        num = jnp.einsum('bhts,bhsv->bhtv', s, vc) + w_inter[..., None] * jnp.einsum('bhvd,bhtd->bhtv', C, qc)
        den = jnp.sum(s, axis=-1) + w_inter * jnp.einsum('bhd,bhtd->bht', n, qc)
        den = jnp.maximum(jnp.abs(den), jnp.exp(-m_out))
        h = num / den[..., None]
        b_end = b[..., -1]
        g = b_end[..., None] - b + ic
        m_new = jnp.maximum(b_end + m, jnp.max(g, axis=-1))
        wg = jnp.exp(g - m_new[..., None])
        decay = jnp.exp(b_end + m - m_new)
        C_new = decay[..., None, None] * C + jnp.einsum('bhs,bhsv,bhsd->bhvd', wg, vc, kc)
        n_new = decay[..., None] * n + jnp.einsum('bhs,bhsd->bhd', wg, kc)
        return (C_new, n_new, m_new), h

    final, hs = lax.scan(step, state, (chunks(q), chunks(k), chunks(v), chunks(i_pre), chunks(logf)))
    return jnp.moveaxis(hs, 0, 2).reshape(B, H, T, dv), final


def mlstm_mixer(hc, hx, w_in, b_gate, g_norm, w_out, ctx_out):
    B = hx.shape[0]

    def project(h):
        T = h.shape[1]
        q, k, v, o, gates = jnp.split(h @ w_in, [ML_QK, 2 * ML_QK, 2 * ML_QK + ML_V, 2 * ML_QK + 2 * ML_V], axis=-1)

        def heads(a, d):
            return a.reshape(B, T, ML_HEADS, d).transpose(0, 2, 1, 3).astype(jnp.float32)

        gates = gates.reshape(B, T, 4, ML_HEADS).astype(jnp.float32) + b_gate.astype(jnp.float32)
        gates = GATE_CAP * jnp.tanh(gates / GATE_CAP)
        gates = jnp.transpose(gates, (2, 0, 3, 1))
        return (heads(q, ML_DQK) * ML_DQK ** -0.5, heads(k, ML_DQK), heads(v, ML_DV), o,
                gates[0::2], jax.nn.log_sigmoid(gates[1::2]))

    def bidir(h, init):
        q, k, v, o, ig, lf = project(h)
        flip = lambda a: jnp.flip(a, axis=2)
        h_f, st_f = mlstm_scan(q, k, v, ig[0], lf[0], init[0])
        h_b, st_b = mlstm_scan(flip(q), flip(k), flip(v), flip(ig[1]), flip(lf[1]), init[1])
        return h_f + flip(h_b), o, (st_f, st_b)

    def readout(hsum, o, dtype):
        T = hsum.shape[2]
        hh = rms_norm(hsum.transpose(0, 2, 1, 3), g_norm.reshape(ML_HEADS, ML_DV))
        hh = hh.reshape(B, T, ML_V).astype(dtype)
        return (jax.nn.sigmoid(o) * hh) @ w_out

    zero = (jnp.zeros((B, ML_HEADS, ML_DV, ML_DQK), jnp.float32),
            jnp.zeros((B, ML_HEADS, ML_DQK), jnp.float32),
            jnp.zeros((B, ML_HEADS), jnp.float32))
    h_c, o_c, ctx_states = bidir(hc, (zero, zero))
    h_l, o_l, _ = bidir(hx, ctx_states)
    y_lat = readout(h_l, o_l, hx.dtype)
    y_ctx = readout(h_c, o_c, hc.dtype) if ctx_out else None
    return y_lat, y_ctx


def softmax_attend(q, k, v):
    s = jnp.einsum('bqhd,bkhd->bhqk', q, k).astype(jnp.float32) * MLA_SCALE
    p = jax.nn.softmax(s, axis=-1).astype(v.dtype)
    return jnp.einsum('bhqk,bkhd->bqhd', p, v)


def mla_mixer(hc, hx, w_down, q_norm, w_uq, kv_norm, w_ukv, w_out, rope, ctx_out):
    B, N, _ = hx.shape
    H = MLA_HEADS

    def queries(cq, rot):
        q = (rms_norm(cq, q_norm) @ w_uq).reshape(*cq.shape[:2], H, MLA_NOPE + MLA_ROPE)
        q_rope = apply_axial_rope(q[..., MLA_NOPE:], rope) if rot else q[..., MLA_NOPE:]
        return jnp.concatenate([q[..., :MLA_NOPE], q_rope], axis=-1)

    def keys_values(ckv, kr, rot):
        kv = (rms_norm(ckv, kv_norm) @ w_ukv).reshape(*ckv.shape[:2], H, MLA_NOPE + MLA_V)
        k_nope, v = jnp.split(kv, [MLA_NOPE], axis=-1)
        kr = kr[:, :, None, :]
        if rot:
            kr = apply_axial_rope(kr, rope)
        k = jnp.concatenate([k_nope, jnp.broadcast_to(kr, (*k_nope.shape[:3], MLA_ROPE))], axis=-1)
        return k, v

    cq_l, ckv_l, kr_l = jnp.split(hx @ w_down, [Q_LORA, Q_LORA + KV_LORA], axis=-1)
    q_l = queries(cq_l, True)
    k_l, v_l = keys_values(ckv_l, kr_l, True)
    d_ctx = hc @ (w_down if ctx_out else w_down[:, Q_LORA:])
    ckv_c, kr_c = jnp.split(d_ctx[..., -(KV_LORA + MLA_ROPE):], [KV_LORA], axis=-1)
    k_c, v_c = keys_values(ckv_c, kr_c, False)

    k_all = jnp.concatenate([k_c, k_l], axis=1)
    v_all = jnp.concatenate([v_c, v_l], axis=1)
    nb = N // Q_BLOCK
    qb = jnp.moveaxis(q_l.reshape(B, nb, Q_BLOCK, H, MLA_NOPE + MLA_ROPE), 1, 0)
    o_l = lax.map(lambda qblk: softmax_attend(qblk, k_all, v_all), qb)
    o_l = jnp.moveaxis(o_l, 0, 1).reshape(B, N, H * MLA_V)
    y_lat = o_l @ w_out
    y_ctx = None
    if ctx_out:
        q_c = queries(d_ctx[..., :Q_LORA], False)
        o_c = softmax_attend(q_c, k_c, v_c).reshape(B, hc.shape[1], H * MLA_V)
        y_ctx = o_c @ w_out
    return y_lat, y_ctx


def conv_ffn(h, w_up, conv_w, conv_b, w_down):
    u = h @ w_up
    up = jnp.pad(u, ((0, 0), (1, 1), (0, 0)))
    u = up[:, :-2] * conv_w[0] + up[:, 1:-1] * conv_w[1] + up[:, 2:] * conv_w[2] + conv_b
    gate, val = jnp.split(u, 2, axis=-1)
    return (jax.nn.silu(gate) * val) @ w_down


def setup_inputs(seed: int = 0) -> dict:
    key = jax.random.key(seed)
    ks = jax.random.split(key, 21)
    D = D_MODEL

    def normal(k, shape, scale):
        return jax.random.normal(k, shape, jnp.float32) * scale

    def gain(k, shape):
        return 1.0 + 0.05 * jax.random.normal(k, shape, jnp.float32)

    gate_base = jnp.array([-1.0, 3.0, -1.0, 3.0], jnp.float32)[None, :, None]
    return {
        'x': normal(ks[0], (BATCH, SEQ, D), 1.0),
        'c': normal(ks[1], (BATCH, D), 1.0),
        'ctx': normal(ks[2], (BATCH, CTX_LEN, D), 1.0),
        'c_ctx': normal(ks[3], (D,), 1.0),
        'mod_w': normal(ks[4], (DEPTH, D, 6 * D), D ** -0.5),
        'mod_b': normal(ks[5], (DEPTH, 6 * D), 0.02),
        'norm_w': gain(ks[6], (DEPTH, 4, D)),
        'ml_w_in': normal(ks[7], (N_MLSTM, D, ML_IN), D ** -0.5),
        'ml_b_gate': gate_base + 0.3 * normal(ks[8], (N_MLSTM, 4, ML_HEADS), 1.0),
        'ml_norm': gain(ks[9], (N_MLSTM, ML_V)),
        'ml_w_out': normal(ks[10], (N_MLSTM, ML_V, D), ML_V ** -0.5),
        'mla_w_down': normal(ks[11], (N_MLA, D, MLA_DOWN), D ** -0.5),
        'mla_q_norm': gain(ks[12], (N_MLA, Q_LORA)),
        'mla_w_uq': normal(ks[13], (N_MLA, Q_LORA, MLA_HEADS * (MLA_NOPE + MLA_ROPE)), Q_LORA ** -0.5),
        'mla_kv_norm': gain(ks[14], (N_MLA, KV_LORA)),
        'mla_w_ukv': normal(ks[15], (N_MLA, KV_LORA, MLA_HEADS * (MLA_NOPE + MLA_V)), KV_LORA ** -0.5),
        'mla_w_out': normal(ks[16], (N_MLA, MLA_HEADS * MLA_V, D), (MLA_HEADS * MLA_V) ** -0.5),
        'ffn_w_up': normal(ks[17], (DEPTH, D, 2 * D_FF), D ** -0.5),
        'ffn_conv_w': normal(ks[18], (DEPTH, 3, 2 * D_FF), 3 ** -0.5),
        'ffn_conv_b': normal(ks[19], (DEPTH, 2 * D_FF), 0.02),
        'ffn_w_down': normal(ks[20], (DEPTH, D_FF, D), D_FF ** -0.5),
    }


def reference(x, c, ctx, c_ctx, mod_w, mod_b, norm_w, ml_w_in, ml_b_gate, ml_norm, ml_w_out,
              mla_w_down, mla_q_norm, mla_w_uq, mla_kv_norm, mla_w_ukv, mla_w_out,
              ffn_w_up, ffn_conv_w, ffn_conv_b, ffn_w_down):
    rope = axial_rope_tables(x.shape[1])
    h_lat, h_ctx = x, ctx
    for i in range(DEPTH):
        ctx_out = i < DEPTH - 1
        j = i // N_MIXERS
        g_pre_a, g_post_a, g_pre_f, g_post_f = norm_w[i, 0], norm_w[i, 1], norm_w[i, 2], norm_w[i, 3]
        sh_a, sc_a, gt_a, sh_f, sc_f, gt_f = adaln(c, mod_w[i], mod_b[i], 6)
        if ctx_out:
            csh_a, csc_a, cgt_a, csh_f, csc_f, cgt_f = adaln(c_ctx, mod_w[i], mod_b[i], 6)
        else:
            csh_a, csc_a = adaln(c_ctx, mod_w[i], mod_b[i], 2)
        a_lat = modulate(h_lat, g_pre_a, sh_a, sc_a)
        a_ctx = modulate(h_ctx, g_pre_a, csh_a, csc_a)
        if i % N_MIXERS == 0:
            y_lat, y_ctx = mlstm_mixer(a_ctx, a_lat, ml_w_in[j], ml_b_gate[j], ml_norm[j], ml_w_out[j], ctx_out)
        else:
            y_lat, y_ctx = mla_mixer(a_ctx, a_lat, mla_w_down[j], mla_q_norm[j], mla_w_uq[j], mla_kv_norm[j],
                                     mla_w_ukv[j], mla_w_out[j], rope, ctx_out)
        h_lat = h_lat + gt_a * rms_norm(y_lat, g_post_a)
        f_lat = conv_ffn(modulate(h_lat, g_pre_f, sh_f, sc_f), ffn_w_up[i], ffn_conv_w[i], ffn_conv_b[i], ffn_w_down[i])
        h_lat = h_lat + gt_f * rms_norm(f_lat, g_post_f)
        if ctx_out:
            h_ctx = h_ctx + cgt_a * rms_norm(y_ctx, g_post_a)
            f_ctx = conv_ffn(modulate(h_ctx, g_pre_f, csh_f, csc_f), ffn_w_up[i], ffn_conv_w[i], ffn_conv_b[i], ffn_w_down[i])
            h_ctx = h_ctx + cgt_f * rms_norm(f_ctx, g_post_f)
    return h_lat
```

```python
import functools

import jax
import jax.numpy as jnp
from jax import lax
from jax.experimental import pallas as pl
from jax.experimental.pallas import tpu as pltpu

F32 = jnp.float32
BF16 = jnp.bfloat16

LANE = 128
VMEM_LIMIT_BYTES = 56 * 1024 * 1024

EPS = 1e-6
ML_HEADS = 8
GATE_CAP = 15.0
GRID_W = 64
ROPE_THETA = 10000.0
MLA_NOPE = 128
MLA_ROPE = 64
MLA_V = 128
MLA_SCALE = (MLA_NOPE + MLA_ROPE) ** -0.5
MLA_QW = 2 * LANE
N_MOD = 6
MOD_ROWS = 8

ROW_TILE = 256


def _params(*sem):
    return pltpu.CompilerParams(dimension_semantics=sem, vmem_limit_bytes=VMEM_LIMIT_BYTES)


def _mod_row(t, tiles_per_batch):
    return jnp.where(t % tiles_per_batch == 0, 0, 1 + t // tiles_per_batch)


def _adaln_kernel(c_ref, w_ref, b_ref, o_ref):
    c = c_ref[...]
    s = c * jax.nn.sigmoid(c)
    o_ref[0] = jnp.dot(s, w_ref[0], preferred_element_type=F32) + b_ref[0]


def adaln_all(cond, mod_w, mod_b, tn=1024):
    depth, d, n = mod_w.shape
    return pl.pallas_call(
        _adaln_kernel,
        grid=(depth, n // tn),
        in_specs=[
            pl.BlockSpec((MOD_ROWS, d), lambda l, j: (0, 0)),
            pl.BlockSpec((1, d, tn), lambda l, j: (l, 0, j)),
            pl.BlockSpec((1, 1, tn), lambda l, j: (l, 0, j)),
        ],
        out_specs=pl.BlockSpec((1, MOD_ROWS, tn), lambda l, j: (l, 0, j)),
        out_shape=jax.ShapeDtypeStruct((depth, MOD_ROWS, n), F32),
        compiler_params=_params("arbitrary", "arbitrary"),
        name="adaln",
    )(cond, mod_w, mod_b.reshape(depth, 1, n))


def _rms(x, g):
    return x * lax.rsqrt(jnp.mean(x * x, axis=-1, keepdims=True) + EPS) * g


def _norm_kernel(*refs, has_y, has_a):
    it = iter(refs)
    h_ref = next(it)
    if has_y:
        y_ref, gate_ref, gpost_ref = next(it), next(it), next(it)
    if has_a:
        gpre_ref, shift_ref, scale_ref = next(it), next(it), next(it)
    if has_y:
        hout_ref = next(it)
    if has_a:
        a_ref = next(it)
    h = h_ref[...]
    if has_y:
        h = h + gate_ref[0] * _rms(y_ref[...], gpost_ref[0])
        hout_ref[...] = h
    if has_a:
        a = _rms(h, gpre_ref[0]) * (1.0 + scale_ref[0]) + shift_ref[0]
        a_ref[...] = a.astype(a_ref.dtype)


def norm_modulate(h, mods, norm_w, layer, *, tiles_per_batch, y=None, y_layer=None, gate_idx=None, post_idx=None,
                  pre_idx=None, shift_idx=None, scale_idx=None, latent_only_out=False):
    r, d = h.shape
    has_y, has_a = y is not None, pre_idx is not None
    y_layer = layer if y_layer is None else y_layer
    if latent_only_out:
        lat_per_batch = tiles_per_batch - 1
        n_tiles = (r // ROW_TILE) // tiles_per_batch * lat_per_batch

        def full_tile(t):
            return (t // lat_per_batch) * tiles_per_batch + 1 + t % lat_per_batch
    else:
        n_tiles = r // ROW_TILE

        def full_tile(t):
            return t

    def row_spec():
        return pl.BlockSpec((ROW_TILE, d), lambda t: (full_tile(t), 0))

    def mod_spec(lyr, which):
        return pl.BlockSpec(
            (1, 1, d), lambda t: ((lyr * MOD_ROWS + _mod_row(full_tile(t), tiles_per_batch)) * N_MOD + which, 0, 0))

    def gain_spec(lyr, which):
        return pl.BlockSpec((1, 1, d), lambda t: (lyr * 4 + which, 0, 0))

    args, in_specs = [h], [row_spec()]
    if has_y:
        args += [y, mods, norm_w]
        in_specs += [row_spec(), mod_spec(y_layer, gate_idx), gain_spec(y_layer, post_idx)]
    if has_a:
        args += [norm_w, mods, mods]
        in_specs += [gain_spec(layer, pre_idx), mod_spec(layer, shift_idx), mod_spec(layer, scale_idx)]
    out_shape, out_specs = [], []
    if has_y:
        out_shape.append(jax.ShapeDtypeStruct((n_tiles * ROW_TILE, d), F32))
        out_specs.append(pl.BlockSpec((ROW_TILE, d), lambda t: (t, 0)))
    if has_a:
        out_shape.append(jax.ShapeDtypeStruct((r, d), BF16))
        out_specs.append(row_spec())
    return pl.pallas_call(
        functools.partial(_norm_kernel, has_y=has_y, has_a=has_a),
        grid=(n_tiles,),
        in_specs=in_specs,
        out_specs=out_specs,
        out_shape=out_shape,
        compiler_params=_params("arbitrary"),
        name="norm_modulate",
    )(*args)


def _rope_lanes(x, cos, sin):
    lane = lax.broadcasted_iota(jnp.int32, x.shape, 1)
    partner = jnp.where(lane % 32 < 16, pltpu.roll(x, LANE - 16, 1), pltpu.roll(x, 16, 1))
    return x * cos + partner * sin


def _mm_kernel(*refs, rope_q):
    if rope_q:
        a_ref, w_ref, cos_ref, sin_ref, o_ref = refs
    else:
        a_ref, w_ref, o_ref = refs
    acc = jnp.dot(a_ref[...], w_ref[...].astype(BF16), preferred_element_type=F32)
    if rope_q:
        cos, sin = cos_ref[...], sin_ref[...]
        for g in range(acc.shape[1] // LANE):
            blk = acc[:, g * LANE:(g + 1) * LANE]
            if g % 2 == 1:
                blk = _rope_lanes(blk, cos, sin)
            o_ref[:, g * LANE:(g + 1) * LANE] = (blk * MLA_SCALE).astype(o_ref.dtype)
    else:
        o_ref[...] = acc.astype(o_ref.dtype)


def _pick_tm(r, cap):
    best = 16
    for tm in range(16, cap + 1, 16):
        if r % tm == 0:
            best = tm
    return best


def matmul(a, w, *, layer=None, n_out=None, tn=512, tm_cap=1088, out_dtype=F32, rope=None):
    r, k = a.shape
    n_out = w.shape[-1] if n_out is None else n_out
    tn = min(tn, n_out)
    assert n_out % tn == 0 and w.shape[-2] == k
    tm = _pick_tm(r, tm_cap)
    if layer is None:
        w_spec = pl.BlockSpec((k, tn), lambda i, j: (0, j))
    else:
        w_spec = pl.BlockSpec((None, k, tn), lambda i, j: (layer, 0, j))
    args = [a, w]
    in_specs = [pl.BlockSpec((tm, k), lambda i, j: (i, 0)), w_spec]
    if rope is not None:
        args += list(rope)
        in_specs += [pl.BlockSpec((tm, LANE), lambda i, j: (i, 0))] * 2
    return pl.pallas_call(
        functools.partial(_mm_kernel, rope_q=rope is not None),
        grid=(r // tm, n_out // tn),
        in_specs=in_specs,
        out_specs=pl.BlockSpec((tm, tn), lambda i, j: (i, j)),
        out_shape=jax.ShapeDtypeStruct((r, n_out), out_dtype),
        compiler_params=_params("arbitrary", "arbitrary"),
        name="matmul",
    )(*args)


def _col_to_row(col, eye):
    return jnp.sum(jnp.where(eye, col, 0.0), axis=0, keepdims=True)


def _mlstm_chunk(q, k, v, ig, lf, mask, eye, c_ref, n_ref, m_ref):
    neg_inf = -jnp.inf
    lf_row = _col_to_row(lf, eye)
    b = jnp.sum(jnp.where(mask, lf_row, 0.0), axis=1, keepdims=True)
    b_end = jnp.sum(lf, axis=0, keepdims=True)
    src = ig - b
    dmat = jnp.where(mask, b + _col_to_row(src, eye), neg_inf)
    m_prev = m_ref[...]
    inter = b + m_prev
    m_out = jnp.maximum(inter, jnp.max(dmat, axis=1, keepdims=True))
    w_intra = jnp.exp(dmat - m_out)
    w_inter = jnp.exp(inter - m_out)
    qk = lax.dot_general(q, k, (((1,), (1,)), ((), ())), preferred_element_type=F32)
    s = qk * w_intra
    c_prev = c_ref[...]
    num = jnp.dot(s.astype(BF16), v, preferred_element_type=F32)
    num = num + w_inter * jnp.dot(q, c_prev.astype(BF16), preferred_element_type=F32)
    qn = jnp.sum(q.astype(F32) * n_ref[...], axis=1, keepdims=True)
    den = jnp.sum(s, axis=1, keepdims=True) + w_inter * qn
    den = jnp.maximum(jnp.abs(den), jnp.exp(-m_out))
    h = num / den
    g = b_end + src
    m_new = jnp.maximum(b_end + m_prev, jnp.max(g, axis=0, keepdims=True))
    wg = jnp.exp(g - m_new)
    decay = jnp.exp(b_end + m_prev - m_new)
    kw = k.astype(F32) * wg
    c_ref[...] = decay * c_prev + lax.dot_general(
        kw.astype(BF16), v, (((0,), (0,)), ((), ())), preferred_element_type=F32)
    n_ref[...] = decay * n_ref[...] + jnp.sum(kw, axis=0, keepdims=True)
    m_ref[...] = m_new
    return h


def _mlstm_kernel(qf_ref, kf_ref, vf_ref, gf_ref, qb_ref, kb_ref, vb_ref, gb_ref, bias_ref,
                  hf_ref, hb_ref, cf_ref, nf_ref, mf_ref, cb_ref, nb_ref, mb_ref, *, dk_scale):
    head = pl.program_id(1)

    @pl.when(pl.program_id(2) == 0)
    def _():
        for ref in (cf_ref, nf_ref, mf_ref, cb_ref, nb_ref, mb_ref):
            ref[...] = jnp.zeros_like(ref)

    chunk = qf_ref.shape[0]
    row = lax.broadcasted_iota(jnp.int32, (chunk, chunk), 0)
    col = lax.broadcasted_iota(jnp.int32, (chunk, chunk), 1)
    eye = row == col
    lane = lax.broadcasted_iota(jnp.int32, gf_ref.shape, 1)

    def gate_cols(g_ref, first):
        g = g_ref[...] + bias_ref[...]
        g = GATE_CAP * jnp.tanh(g / GATE_CAP)
        ig = jnp.sum(jnp.where(lane == first * ML_HEADS + head, g, 0.0), axis=1, keepdims=True)
        fg = jnp.sum(jnp.where(lane == (first + 1) * ML_HEADS + head, g, 0.0), axis=1, keepdims=True)
        return ig, jax.nn.log_sigmoid(fg)

    def run(q_ref, k_ref, v_ref, g_ref, first, mask, c_ref, n_ref, m_ref, h_ref):
        ig, lf = gate_cols(g_ref, first)
        q = (q_ref[...] * dk_scale).astype(BF16)
        h = _mlstm_chunk(q, k_ref[...].astype(BF16), v_ref[...].astype(BF16), ig, lf, mask, eye, c_ref, n_ref, m_ref)
        h_ref[...] = h

    run(qf_ref, kf_ref, vf_ref, gf_ref, 0, col <= row, cf_ref, nf_ref, mf_ref, hf_ref)
    run(qb_ref, kb_ref, vb_ref, gb_ref, 2, col >= row, cb_ref, nb_ref, mb_ref, hb_ref)


def mlstm_scan(qkvo, gates, gate_bias, *, batch, tiles_per_batch, dk, dv):
    r = qkvo.shape[0]
    tpb = tiles_per_batch

    def fwd_tile(b, c):
        return b * tpb + c

    def bwd_tile(b, c):
        return b * tpb + jnp.where(c == 0, 0, tpb - c)

    def specs(tile):
        return [
            pl.BlockSpec((ROW_TILE, dk), lambda b, h, c: (tile(b, c), h)),
            pl.BlockSpec((ROW_TILE, dk), lambda b, h, c: (tile(b, c), ML_HEADS + h)),
            pl.BlockSpec((ROW_TILE, dv), lambda b, h, c: (tile(b, c), (2 * ML_HEADS * dk) // dv + h)),
            pl.BlockSpec((ROW_TILE, LANE), lambda b, h, c: (tile(b, c), 0)),
        ]

    def out_spec(tile):
        return pl.BlockSpec((ROW_TILE, dv), lambda b, h, c: (tile(b, c), h))

    state = [pltpu.VMEM((dk, dv), F32), pltpu.VMEM((1, dk), F32), pltpu.VMEM((1, 1), F32)]
    return pl.pallas_call(
        functools.partial(_mlstm_kernel, dk_scale=dk ** -0.5),
        grid=(batch, ML_HEADS, tpb),
        in_specs=specs(fwd_tile) + specs(bwd_tile) + [pl.BlockSpec((1, LANE), lambda b, h, c: (0, 0))],
        out_specs=[out_spec(fwd_tile), out_spec(bwd_tile)],
        out_shape=[jax.ShapeDtypeStruct((r, ML_HEADS * dv), F32)] * 2,
        scratch_shapes=state + state,
        compiler_params=_params("arbitrary", "arbitrary", "arbitrary"),
        name="mlstm_scan",
    )(qkvo, qkvo, qkvo, gates, qkvo, qkvo, qkvo, gates, gate_bias)


def _readout_kernel(hf_ref, hb_ref, o_ref, g_ref, out_ref, *, dv):
    for hd in range(ML_HEADS):
        sl = slice(hd * dv, (hd + 1) * dv)
        hs = hf_ref[:, sl] + hb_ref[:, sl]
        o = o_ref[:, sl]
        out_ref[:, sl] = (jax.nn.sigmoid(o) * _rms(hs, g_ref[:, sl])).astype(out_ref.dtype)


def mlstm_readout(h_f, h_b, qkvo, g_norm, *, dv):
    r, width = h_f.shape
    o_block = (qkvo.shape[1] - width) // width
    row = pl.BlockSpec((ROW_TILE, width), lambda t: (t, 0))
    return pl.pallas_call(
        functools.partial(_readout_kernel, dv=dv),
        grid=(r // ROW_TILE,),
        in_specs=[row, row, pl.BlockSpec((ROW_TILE, width), lambda t: (t, o_block)),
                  pl.BlockSpec((1, width), lambda t: (0, 0))],
        out_specs=row,
        out_shape=jax.ShapeDtypeStruct((r, width), BF16),
        compiler_params=_params("arbitrary"),
        name="mlstm_readout",
    )(h_f, h_b, qkvo, g_norm)


def _mla_prep_kernel(down_ref, qn_ref, kvn_ref, cos_ref, sin_ref, cq_ref, ckv_ref, kr_ref, *, q_lora, kv_lora):
    cq_ref[...] = _rms(down_ref[:, :q_lora], qn_ref[...]).astype(cq_ref.dtype)
    ckv_ref[...] = _rms(down_ref[:, q_lora:q_lora + kv_lora], kvn_ref[...]).astype(ckv_ref.dtype)
    kr = down_ref[:, q_lora + kv_lora:q_lora + kv_lora + LANE]
    lane = lax.broadcasted_iota(jnp.int32, kr.shape, 1)
    kr = jnp.where(lane < MLA_ROPE, kr, 0.0)
    kr_ref[...] = _rope_lanes(kr, cos_ref[...], sin_ref[...]).astype(kr_ref.dtype)


def mla_prep(down, q_norm, kv_norm, cos, sin, *, q_lora, kv_lora):
    r, width = down.shape
    return pl.pallas_call(
        functools.partial(_mla_prep_kernel, q_lora=q_lora, kv_lora=kv_lora),
        grid=(r // ROW_TILE,),
        in_specs=[pl.BlockSpec((ROW_TILE, width), lambda t: (t, 0)),
                  pl.BlockSpec((1, q_lora), lambda t: (0, 0)),
                  pl.BlockSpec((1, kv_lora), lambda t: (0, 0)),
                  pl.BlockSpec((ROW_TILE, LANE), lambda t: (t, 0)),
                  pl.BlockSpec((ROW_TILE, LANE), lambda t: (t, 0))],
        out_specs=[pl.BlockSpec((ROW_TILE, q_lora), lambda t: (t, 0)),
                   pl.BlockSpec((ROW_TILE, kv_lora), lambda t: (t, 0)),
                   pl.BlockSpec((ROW_TILE, LANE), lambda t: (t, 0))],
        out_shape=[jax.ShapeDtypeStruct((r, q_lora), BF16),
                   jax.ShapeDtypeStruct((r, kv_lora), BF16),
                   jax.ShapeDtypeStruct((r, LANE), BF16)],
        compiler_params=_params("arbitrary"),
        name="mla_prep",
    )(down, q_norm, kv_norm, cos, sin)


def _attn_kernel(q_ref, kn_ref, kr_ref, v_ref, o_ref, kcat_ref, *, ctx_len, tk):
    qi = pl.program_id(2)

    @pl.when(qi == 0)
    def _():
        kcat_ref[:, :LANE] = kn_ref[...]
        kcat_ref[:, LANE:] = kr_ref[...]

    q = q_ref[...]
    tq = q.shape[0]

    def block(carry, k, v):
        m, l, acc = carry
        s = lax.dot_general(q, k, (((1,), (1,)), ((), ())), preferred_element_type=F32)
        m_new = jnp.maximum(m, jnp.max(s, axis=1, keepdims=True))
        alpha = jnp.exp(m - m_new)
        p = jnp.exp(s - m_new)
        l = alpha * l + jnp.sum(p, axis=1, keepdims=True)
        acc = alpha * acc + jnp.dot(p.astype(BF16), v, preferred_element_type=F32)
        return m_new, l, acc

    init = (jnp.full((tq, 1), -jnp.inf, F32), jnp.zeros((tq, 1), F32), jnp.zeros((tq, MLA_V), F32))
    carry = block(init, kcat_ref[:ctx_len, :], v_ref[:ctx_len, :])

    def finish(c):
        _, l, acc = c
        o_ref[...] = (acc / l).astype(o_ref.dtype)

    @pl.when(qi == 0)
    def _():
        finish(carry)

    @pl.when(qi != 0)
    def _():
        def body(j, c):
            start = pl.multiple_of(ctx_len + j * tk, tk // 2)
            return block(c, kcat_ref[pl.ds(start, tk), :], v_ref[pl.ds(start, tk), :])

        finish(lax.fori_loop(0, (kcat_ref.shape[0] - ctx_len) // tk, body, carry))


def mla_attention(q, kv, kr, *, batch, heads, tiles_per_batch, tk=512):
    r = q.shape[0]
    rows_pb = tiles_per_batch * ROW_TILE
    assert (rows_pb - ROW_TILE) % tk == 0
    return pl.pallas_call(
        functools.partial(_attn_kernel, ctx_len=ROW_TILE, tk=tk),
        grid=(batch, heads, tiles_per_batch),
        in_specs=[
            pl.BlockSpec((ROW_TILE, MLA_QW), lambda b, h, i: (b * tiles_per_batch + i, h)),
            pl.BlockSpec((rows_pb, LANE), lambda b, h, i: (b, 2 * h)),
            pl.BlockSpec((rows_pb, LANE), lambda b, h, i: (b, 0)),
            pl.BlockSpec((rows_pb, LANE), lambda b, h, i: (b, 2 * h + 1)),
        ],
        out_specs=pl.BlockSpec((ROW_TILE, MLA_V), lambda b, h, i: (b * tiles_per_batch + i, h)),
        out_shape=jax.ShapeDtypeStruct((r, heads * MLA_V), BF16),
        scratch_shapes=[pltpu.VMEM((rows_pb, MLA_QW), BF16)],
        compiler_params=_params("arbitrary", "arbitrary", "arbitrary"),
        name="mla_attention",
    )(q, kv, kr, kv)


def _conv_gate_kernel(ug_ref, uv_ref, pg_ref, pv_ref, ng_ref, nv_ref, wg_ref, wv_ref, bg_ref, bv_ref, o_ref,
                      *, tiles_per_batch):
    t = pl.program_id(0) % tiles_per_batch
    first = jnp.logical_or(t == 0, t == 1)
    last = jnp.logical_or(t == 0, t == tiles_per_batch - 1)
    rows = ug_ref.shape[0]
    ridx = lax.broadcasted_iota(jnp.int32, ug_ref.shape, 0)

    def conv(u_ref, p_ref, n_ref, w_ref, b_ref):
        u = u_ref[...]
        prev_row = jnp.where(first, 0.0, p_ref[7:8, :])
        next_row = jnp.where(last, 0.0, n_ref[0:1, :])
        up = jnp.where(ridx == 0, prev_row, pltpu.roll(u, 1, 0))
        un = jnp.where(ridx == rows - 1, next_row, pltpu.roll(u, rows - 1, 0))
        return up * w_ref[0:1, :] + u * w_ref[1:2, :] + un * w_ref[2:3, :] + b_ref[...]

    gate = conv(ug_ref, pg_ref, ng_ref, wg_ref, bg_ref)
    val = conv(uv_ref, pv_ref, nv_ref, wv_ref, bv_ref)
    o_ref[...] = (gate * jax.nn.sigmoid(gate) * val).astype(o_ref.dtype)


def conv_gate(u, conv_w, conv_b, *, tiles_per_batch, tn=1024):
    r, two_f = u.shape
    f = two_f // 2
    nb = f // tn
    sub = 8
    per_tile = ROW_TILE // sub
    last_sub = r // sub - 1

    def main(off):
        return pl.BlockSpec((ROW_TILE, tn), lambda t, j: (t, off + j))

    def prev(off):
        return pl.BlockSpec((sub, tn), lambda t, j: (jnp.maximum(t * per_tile - 1, 0), off + j))

    def nxt(off):
        return pl.BlockSpec((sub, tn), lambda t, j: (jnp.minimum((t + 1) * per_tile, last_sub), off + j))

    def wspec(off):
        return pl.BlockSpec((3, tn), lambda t, j: (0, off + j))

    def bspec(off):
        return pl.BlockSpec((1, tn), lambda t, j: (0, off + j))

    return pl.pallas_call(
        functools.partial(_conv_gate_kernel, tiles_per_batch=tiles_per_batch),
        grid=(r // ROW_TILE, nb),
        in_specs=[main(0), main(nb), prev(0), prev(nb), nxt(0), nxt(nb), wspec(0), wspec(nb), bspec(0), bspec(nb)],
        out_specs=pl.BlockSpec((ROW_TILE, tn), lambda t, j: (t, j)),
        out_shape=jax.ShapeDtypeStruct((r, f), BF16),
        compiler_params=_params("arbitrary", "arbitrary"),
        name="conv_gate",
    )(u, u, u, u, u, u, conv_w, conv_w, conv_b.reshape(1, two_f), conv_b.reshape(1, two_f))


def _rope_tables(batch, ctx_len, seq):
    per_axis = MLA_ROPE // 2
    inv_freq = ROPE_THETA ** (-jnp.arange(0, per_axis, 2, dtype=F32) / per_axis)
    pos = jnp.arange(seq)
    ang_r = (pos // GRID_W).astype(F32)[:, None] * inv_freq
    ang_c = (pos % GRID_W).astype(F32)[:, None] * inv_freq
    cos64 = jnp.concatenate([jnp.cos(ang_r)] * 2 + [jnp.cos(ang_c)] * 2, axis=1)
    sin64 = jnp.concatenate([-jnp.sin(ang_r), jnp.sin(ang_r), -jnp.sin(ang_c), jnp.sin(ang_c)], axis=1)
    cos_lat = jnp.concatenate([cos64, cos64], axis=1)
    sin_lat = jnp.concatenate([sin64, sin64], axis=1)
    cos_b = jnp.concatenate([jnp.ones((ctx_len, LANE), F32), cos_lat], axis=0)
    sin_b = jnp.concatenate([jnp.zeros((ctx_len, LANE), F32), sin_lat], axis=0)
    return jnp.tile(cos_b, (batch, 1)), jnp.tile(sin_b, (batch, 1))


def kernel(x, c, ctx, c_ctx, mod_w, mod_b, norm_w, ml_w_in, ml_b_gate, ml_norm, ml_w_out, mla_w_down, mla_q_norm,
           mla_w_uq, mla_kv_norm, mla_w_ukv, mla_w_out, ffn_w_up, ffn_conv_w, ffn_conv_b, ffn_w_down):
    batch, seq, d = x.shape
    ctx_len = ctx.shape[1]
    depth = mod_w.shape[0]
    assert ctx_len == ROW_TILE and seq % ROW_TILE == 0 and batch + 1 <= MOD_ROWS
    tpb = (ctx_len + seq) // ROW_TILE
    ml_qk = d // 2
    dk, dv = ml_qk // ML_HEADS, d // ML_HEADS
    heads = d // MLA_NOPE
    q_lora, kv_lora = d // 4, d // 8
    d_ff = ffn_w_down.shape[1]

    h = jnp.concatenate([ctx, x], axis=1).reshape(batch * (ctx_len + seq), d)
    cond = jnp.concatenate([c_ctx[None], c, jnp.zeros((MOD_ROWS - 1 - batch, d), F32)], axis=0)
    mods = adaln_all(cond, mod_w, mod_b).reshape(depth * MOD_ROWS * N_MOD, 1, d)
    gains = norm_w.reshape(depth * 4, 1, d)
    cos, sin = _rope_tables(batch, ctx_len, seq)
    norm = functools.partial(norm_modulate, mods=mods, norm_w=gains, tiles_per_batch=tpb)

    f = None
    for i in range(depth):
        j = i // 2
        if f is None:
            a = norm(h, layer=i, pre_idx=0, shift_idx=0, scale_idx=1)[0]
        else:
            h, a = norm(h, layer=i, y=f, y_layer=i - 1, gate_idx=5, post_idx=3, pre_idx=0, shift_idx=0, scale_idx=1)
        if i % 2 == 0:
            qkvo = matmul(a, ml_w_in, layer=j, n_out=2 * ml_qk + 2 * d)
            w_gate = jnp.pad(ml_w_in[j, :, 2 * ml_qk + 2 * d:], ((0, 0), (0, LANE - 4 * ML_HEADS)))
            gates = matmul(a, w_gate)
            bias = jnp.pad(ml_b_gate[j].reshape(1, 4 * ML_HEADS), ((0, 0), (0, LANE - 4 * ML_HEADS)))
            h_f, h_b = mlstm_scan(qkvo, gates, bias, batch=batch, tiles_per_batch=tpb, dk=dk, dv=dv)
            mixed = mlstm_readout(h_f, h_b, qkvo, ml_norm[j].reshape(1, d), dv=dv)
            y = matmul(mixed, ml_w_out, layer=j)
        else:
            down_w = mla_w_down[j]
            down = matmul(a, jnp.pad(down_w, ((0, 0), (0, -down_w.shape[1] % 512))))
            cq, ckv, kr = mla_prep(down, mla_q_norm[j].reshape(1, q_lora), mla_kv_norm[j].reshape(1, kv_lora),
                                   cos, sin, q_lora=q_lora, kv_lora=kv_lora)
            w_q = jnp.pad(mla_w_uq[j].reshape(q_lora, heads, MLA_NOPE + MLA_ROPE),
                          ((0, 0), (0, 0), (0, MLA_QW - MLA_NOPE - MLA_ROPE))).reshape(q_lora, heads * MLA_QW)
            q = matmul(cq, w_q, out_dtype=BF16, rope=(cos, sin))
            kv = matmul(ckv, mla_w_ukv, layer=j, out_dtype=BF16)
            att = mla_attention(q, kv, kr, batch=batch, heads=heads, tiles_per_batch=tpb)
            y = matmul(att, mla_w_out, layer=j)
        h, a = norm(h, layer=i, y=y, gate_idx=2, post_idx=1, pre_idx=2, shift_idx=3, scale_idx=4)
        u = matmul(a, ffn_w_up, layer=i)
        g = conv_gate(u, ffn_conv_w[i], ffn_conv_b[i], tiles_per_batch=tpb)
        f = matmul(g, ffn_w_down, layer=i, tn=256)
    out = norm(h, layer=depth - 1, y=f, gate_idx=5, post_idx=3, latent_only_out=True)[0]
    return out.reshape(batch, seq, d)
```

```python
import functools
import math

import jax
import jax.numpy as jnp
from jax import lax
from jax.experimental import pallas as pl
from jax.experimental.pallas import tpu as pltpu

F32 = jnp.float32
BF16 = jnp.bfloat16

LANE = 128
SUBLANE = 8
VMEM_LIMIT_BYTES = 56 * 1024 * 1024
W_BLOCK_BYTES = 8 * 1024 * 1024

EPS = 1e-6
ML_HEADS = 8
GATE_CAP = 15.0
GRID_W = 64
ROPE_THETA = 10000.0
MLA_NOPE = 128
MLA_ROPE = 64
MLA_V = 128
MLA_SCALE = (MLA_NOPE + MLA_ROPE) ** -0.5
MLA_QW = 2 * LANE
N_MOD = 6
MOD_ROWS = 8

ROW_TILE = 256


def _params(*sem):
    return pltpu.CompilerParams(dimension_semantics=sem, vmem_limit_bytes=VMEM_LIMIT_BYTES)


class Layout:
    def __init__(self, batch, seq, ctx_len):
        assert ctx_len == ROW_TILE and seq % (2 * ROW_TILE) == 0 and batch + 1 <= MOD_ROWS
        self.batch = batch
        self.seq_tiles = seq // ROW_TILE
        self.lat_tiles = batch * self.seq_tiles
        self.tiles = self.lat_tiles + batch
        self.lat_rows = batch * seq
        self.rows = self.tiles * ROW_TILE

    def mod_row(self, t):
        return jnp.where(t >= self.lat_tiles, 0, 1 + t // self.seq_tiles)

    def seq_first(self, t):
        return jnp.logical_or(t >= self.lat_tiles, t % self.seq_tiles == 0)

    def seq_last(self, t):
        return jnp.logical_or(t >= self.lat_tiles, t % self.seq_tiles == self.seq_tiles - 1)

    def ctx_tile(self, b):
        return self.lat_tiles + b


def _adaln_kernel(c_ref, w_ref, b_ref, o_ref):
    c = c_ref[...]
    s = c * jax.nn.sigmoid(c)
    o_ref[0] = jnp.dot(s, w_ref[0], preferred_element_type=F32) + b_ref[0]


def adaln_all(cond, mod_w, mod_b, tn=1024):
    depth, d, n = mod_w.shape
    return pl.pallas_call(
        _adaln_kernel,
        grid=(depth, n // tn),
        in_specs=[
            pl.BlockSpec((MOD_ROWS, d), lambda l, j: (0, 0)),
            pl.BlockSpec((1, d, tn), lambda l, j: (l, 0, j)),
            pl.BlockSpec((1, 1, tn), lambda l, j: (l, 0, j)),
        ],
        out_specs=pl.BlockSpec((1, MOD_ROWS, tn), lambda l, j: (l, 0, j)),
        out_shape=jax.ShapeDtypeStruct((depth, MOD_ROWS, n), F32),
        compiler_params=_params("arbitrary", "arbitrary"),
        name="adaln",
    )(cond, mod_w, mod_b.reshape(depth, 1, n))


def _rms(x, g):
    return x * lax.rsqrt(jnp.mean(x * x, axis=-1, keepdims=True) + EPS) * g


def _norm_kernel(*refs, has_y, has_a):
    it = iter(refs)
    h_ref = next(it)
    if has_y:
        y_ref, gate_ref, gpost_ref = next(it), next(it), next(it)
    if has_a:
        gpre_ref, shift_ref, scale_ref = next(it), next(it), next(it)
    if has_y:
        hout_ref = next(it)
    if has_a:
        a_ref = next(it)
    h = h_ref[...]
    if has_y:
        h = h + gate_ref[0] * _rms(y_ref[...], gpost_ref[0])
        hout_ref[...] = h
    if has_a:
        a = _rms(h, gpre_ref[0]) * (1.0 + scale_ref[0]) + shift_ref[0]
        a_ref[...] = a.astype(a_ref.dtype)


def norm_modulate(h, mods, norm_w, layer, *, lay, rows, y=None, y_layer=None, gate_idx=None, post_idx=None,
                  pre_idx=None, shift_idx=None, scale_idx=None):
    d = h.shape[1]
    has_y, has_a = y is not None, pre_idx is not None
    y_layer = layer if y_layer is None else y_layer
    row_spec = pl.BlockSpec((ROW_TILE, d), lambda t: (t, 0))

    def mod_spec(lyr, which):
        return pl.BlockSpec((1, 1, d), lambda t: ((lyr * MOD_ROWS + lay.mod_row(t)) * N_MOD + which, 0, 0))

    def gain_spec(lyr, which):
        return pl.BlockSpec((1, 1, d), lambda t: (lyr * 4 + which, 0, 0))

    args, in_specs = [h], [row_spec]
    if has_y:
        args += [y, mods, norm_w]
        in_specs += [row_spec, mod_spec(y_layer, gate_idx), gain_spec(y_layer, post_idx)]
    if has_a:
        args += [norm_w, mods, mods]
        in_specs += [gain_spec(layer, pre_idx), mod_spec(layer, shift_idx), mod_spec(layer, scale_idx)]
    out_shape, out_specs = [], []
    if has_y:
        out_shape.append(jax.ShapeDtypeStruct((rows, d), F32))
        out_specs.append(row_spec)
    if has_a:
        out_shape.append(jax.ShapeDtypeStruct((rows, d), BF16))
        out_specs.append(row_spec)
    return pl.pallas_call(
        functools.partial(_norm_kernel, has_y=has_y, has_a=has_a),
        grid=(rows // ROW_TILE,),
        in_specs=in_specs,
        out_specs=out_specs,
        out_shape=out_shape,
        compiler_params=_params("arbitrary"),
        name="norm_modulate",
    )(*args)


def _rope_lanes(x, cos, sin):
    lane = lax.broadcasted_iota(jnp.int32, x.shape, 1)
    partner = jnp.where(lane % 32 < 16, pltpu.roll(x, LANE - 16, 1), pltpu.roll(x, 16, 1))
    return x * cos + partner * sin


def _mm_kernel(*refs, rope_scale, w_t):
    if rope_scale is not None:
        a_ref, w_ref, cos_ref, sin_ref, o_ref = refs
    else:
        a_ref, w_ref, o_ref = refs
    w = w_ref[...].astype(BF16)
    if w_t:
        acc = lax.dot_general(a_ref[...], w, (((1,), (1,)), ((), ())), preferred_element_type=F32)
    else:
        acc = jnp.dot(a_ref[...], w, preferred_element_type=F32)
    if rope_scale is not None:
        cos, sin = cos_ref[...], sin_ref[...]
        for g in range(acc.shape[1] // LANE):
            blk = acc[:, g * LANE:(g + 1) * LANE]
            if g % 2 == 1:
                blk = _rope_lanes(blk, cos, sin)
            o_ref[:, g * LANE:(g + 1) * LANE] = (blk * rope_scale).astype(o_ref.dtype)
    else:
        o_ref[...] = acc.astype(o_ref.dtype)


def _pick_tm(r, cap):
    return max(tm for tm in range(16, cap + 1, 16) if r % tm == 0)


def _pick_tn(n, k):
    tn = 256
    while n % (2 * tn) == 0 and 2 * tn * k * 4 <= W_BLOCK_BYTES and 2 * tn <= 2048:
        tn *= 2
    return min(tn, n)


def matmul(a, w, *, layer=None, n_out=None, w_t=False, col0=0, tm_cap=1088, out_dtype=F32, rope=None,
           rope_scale=None):
    r, k = a.shape
    n_w, k_w = (w.shape[-2], w.shape[-1]) if w_t else (w.shape[-1], w.shape[-2])
    n_out = n_w if n_out is None else n_out
    tn = _pick_tn(n_out, k)
    assert n_out % tn == 0 and col0 % tn == 0 and k_w == k
    cb0 = col0 // tn
    tm = _pick_tm(r, tm_cap)
    w_block = (tn, k) if w_t else (k, tn)
    w_index = (lambda j: (cb0 + j, 0)) if w_t else (lambda j: (0, cb0 + j))
    if layer is None:
        w_spec = pl.BlockSpec(w_block, lambda i, j: w_index(j))
    else:
        w_spec = pl.BlockSpec((None,) + w_block, lambda i, j: (layer,) + w_index(j))
    args = [a, w]
    in_specs = [pl.BlockSpec((tm, k), lambda i, j: (i, 0)), w_spec]
    if rope is not None:
        args += list(rope)
        in_specs += [pl.BlockSpec((tm, LANE), lambda i, j: (i, 0))] * 2
    return pl.pallas_call(
        functools.partial(_mm_kernel, rope_scale=rope_scale if rope is not None else None, w_t=w_t),
        grid=(r // tm, n_out // tn),
        in_specs=in_specs,
        out_specs=pl.BlockSpec((tm, tn), lambda i, j: (i, j)),
        out_shape=jax.ShapeDtypeStruct((r, n_out), out_dtype),
        compiler_params=_params("arbitrary", "arbitrary"),
        name="matmul",
    )(*args)


def _col_to_row(col, eye):
    return jnp.sum(jnp.where(eye, col, 0.0), axis=0, keepdims=True)


def _mlstm_chunk(q, k, v, ig, lf, mask, eye, c_ref, n_ref, m_ref):
    neg_inf = -jnp.inf
    lf_row = _col_to_row(lf, eye)
    b = jnp.sum(jnp.where(mask, lf_row, 0.0), axis=1, keepdims=True)
    b_end = jnp.sum(lf, axis=0, keepdims=True)
    src = ig - b
    dmat = jnp.where(mask, b + _col_to_row(src, eye), neg_inf)
    m_prev = m_ref[...]
    inter = b + m_prev
    m_out = jnp.maximum(inter, jnp.max(dmat, axis=1, keepdims=True))
    w_intra = jnp.exp(dmat - m_out)
    w_inter = jnp.exp(inter - m_out)
    qk = lax.dot_general(q, k, (((1,), (1,)), ((), ())), preferred_element_type=F32)
    s = qk * w_intra
    c_prev = c_ref[...]
    num = jnp.dot(s.astype(BF16), v, preferred_element_type=F32)
    num = num + w_inter * jnp.dot(q, c_prev.astype(BF16), preferred_element_type=F32)
    qn = jnp.sum(q.astype(F32) * n_ref[...], axis=1, keepdims=True)
    den = jnp.sum(s, axis=1, keepdims=True) + w_inter * qn
    den = jnp.maximum(jnp.abs(den), jnp.exp(-m_out))
    h = num / den
    g = b_end + src
    m_new = jnp.maximum(b_end + m_prev, jnp.max(g, axis=0, keepdims=True))
    wg = jnp.exp(g - m_new)
    decay = jnp.exp(b_end + m_prev - m_new)
    kw = k.astype(F32) * wg
    c_ref[...] = decay * c_prev + lax.dot_general(
        kw.astype(BF16), v, (((0,), (0,)), ((), ())), preferred_element_type=F32)
    n_ref[...] = decay * n_ref[...] + jnp.sum(kw, axis=0, keepdims=True)
    m_ref[...] = m_new
    return h


def _mlstm_kernel(qf_ref, kf_ref, vf_ref, gf_ref, qb_ref, kb_ref, vb_ref, gb_ref, bias_ref,
                  hf_ref, hb_ref, cf_ref, nf_ref, mf_ref, cb_ref, nb_ref, mb_ref, *, dk_scale):
    head = pl.program_id(1)

    @pl.when(pl.program_id(2) == 0)
    def _():
        for ref in (cf_ref, nf_ref, mf_ref, cb_ref, nb_ref, mb_ref):
            ref[...] = jnp.zeros_like(ref)

    chunk = qf_ref.shape[0]
    row = lax.broadcasted_iota(jnp.int32, (chunk, chunk), 0)
    col = lax.broadcasted_iota(jnp.int32, (chunk, chunk), 1)
    eye = row == col
    lane = lax.broadcasted_iota(jnp.int32, gf_ref.shape, 1)

    def gate_cols(g_ref, first):
        g = g_ref[...] + bias_ref[...]
        g = GATE_CAP * jnp.tanh(g / GATE_CAP)
        ig = jnp.sum(jnp.where(lane == first * ML_HEADS + head, g, 0.0), axis=1, keepdims=True)
        fg = jnp.sum(jnp.where(lane == (first + 1) * ML_HEADS + head, g, 0.0), axis=1, keepdims=True)
        return ig, jax.nn.log_sigmoid(fg)

    def run(q_ref, k_ref, v_ref, g_ref, first, mask, c_ref, n_ref, m_ref, h_ref):
        ig, lf = gate_cols(g_ref, first)
        q = (q_ref[...] * dk_scale).astype(BF16)
        h = _mlstm_chunk(q, k_ref[...].astype(BF16), v_ref[...].astype(BF16), ig, lf, mask, eye, c_ref, n_ref, m_ref)
        h_ref[...] = h

    run(qf_ref, kf_ref, vf_ref, gf_ref, 0, col <= row, cf_ref, nf_ref, mf_ref, hf_ref)
    run(qb_ref, kb_ref, vb_ref, gb_ref, 2, col >= row, cb_ref, nb_ref, mb_ref, hb_ref)


def mlstm_scan(qkvo, gates, gate_bias, *, lay, dk, dv):
    r = qkvo.shape[0]
    n_gates = gates.shape[1]
    st = lay.seq_tiles

    def fwd_tile(b, c):
        return jnp.where(c == 0, lay.ctx_tile(b), b * st + c - 1)

    def bwd_tile(b, c):
        return jnp.where(c == 0, lay.ctx_tile(b), b * st + st - c)

    def specs(tile):
        return [
            pl.BlockSpec((ROW_TILE, dk), lambda b, h, c: (tile(b, c), h)),
            pl.BlockSpec((ROW_TILE, dk), lambda b, h, c: (tile(b, c), ML_HEADS + h)),
            pl.BlockSpec((ROW_TILE, dv), lambda b, h, c: (tile(b, c), (2 * ML_HEADS * dk) // dv + h)),
            pl.BlockSpec((ROW_TILE, n_gates), lambda b, h, c: (tile(b, c), 0)),
        ]

    def out_spec(tile):
        return pl.BlockSpec((ROW_TILE, dv), lambda b, h, c: (tile(b, c), h))

    state = [pltpu.VMEM((dk, dv), F32), pltpu.VMEM((1, dk), F32), pltpu.VMEM((1, 1), F32)]
    return pl.pallas_call(
        functools.partial(_mlstm_kernel, dk_scale=dk ** -0.5),
        grid=(lay.batch, ML_HEADS, st + 1),
        in_specs=specs(fwd_tile) + specs(bwd_tile) + [pl.BlockSpec((1, n_gates), lambda b, h, c: (0, 0))],
        out_specs=[out_spec(fwd_tile), out_spec(bwd_tile)],
        out_shape=[jax.ShapeDtypeStruct((r, ML_HEADS * dv), F32)] * 2,
        scratch_shapes=state + state,
        compiler_params=_params("arbitrary", "arbitrary", "arbitrary"),
        name="mlstm_scan",
    )(qkvo, qkvo, qkvo, gates, qkvo, qkvo, qkvo, gates, gate_bias)


def _readout_kernel(hf_ref, hb_ref, o_ref, g_ref, out_ref, *, dv):
    for hd in range(ML_HEADS):
        sl = slice(hd * dv, (hd + 1) * dv)
        hs = hf_ref[:, sl] + hb_ref[:, sl]
        o = o_ref[:, sl]
        out_ref[:, sl] = (jax.nn.sigmoid(o) * _rms(hs, g_ref[:, sl])).astype(out_ref.dtype)


def mlstm_readout(h_f, h_b, qkvo, g_norm, *, dv):
    r, width = h_f.shape
    o_block = (qkvo.shape[1] - width) // width
    row = pl.BlockSpec((ROW_TILE, width), lambda t: (t, 0))
    return pl.pallas_call(
        functools.partial(_readout_kernel, dv=dv),
        grid=(r // ROW_TILE,),
        in_specs=[row, row, pl.BlockSpec((ROW_TILE, width), lambda t: (t, o_block)),
                  pl.BlockSpec((1, width), lambda t: (0, 0))],
        out_specs=row,
        out_shape=jax.ShapeDtypeStruct((r, width), BF16),
        compiler_params=_params("arbitrary"),
        name="mlstm_readout",
    )(h_f, h_b, qkvo, g_norm)


def _mla_prep_kernel(down_ref, qn_ref, kvn_ref, cos_ref, sin_ref, cq_ref, ckv_ref, kr_ref, *, q_lora, kv_lora):
    cq_ref[...] = _rms(down_ref[:, :q_lora], qn_ref[...]).astype(cq_ref.dtype)
    ckv_ref[...] = _rms(down_ref[:, q_lora:q_lora + kv_lora], kvn_ref[...]).astype(ckv_ref.dtype)
    kr = down_ref[:, q_lora + kv_lora:q_lora + kv_lora + LANE]
    lane = lax.broadcasted_iota(jnp.int32, kr.shape, 1)
    kr = jnp.where(lane < MLA_ROPE, kr, 0.0)
    kr_ref[...] = _rope_lanes(kr, cos_ref[...], sin_ref[...]).astype(kr_ref.dtype)


def mla_prep(down, q_norm, kv_norm, cos, sin, *, q_lora, kv_lora):
    r, width = down.shape
    return pl.pallas_call(
        functools.partial(_mla_prep_kernel, q_lora=q_lora, kv_lora=kv_lora),
        grid=(r // ROW_TILE,),
        in_specs=[pl.BlockSpec((ROW_TILE, width), lambda t: (t, 0)),
                  pl.BlockSpec((1, q_lora), lambda t: (0, 0)),
                  pl.BlockSpec((1, kv_lora), lambda t: (0, 0)),
                  pl.BlockSpec((ROW_TILE, LANE), lambda t: (t, 0)),
                  pl.BlockSpec((ROW_TILE, LANE), lambda t: (t, 0))],
        out_specs=[pl.BlockSpec((ROW_TILE, q_lora), lambda t: (t, 0)),
                   pl.BlockSpec((ROW_TILE, kv_lora), lambda t: (t, 0)),
                   pl.BlockSpec((ROW_TILE, LANE), lambda t: (t, 0))],
        out_shape=[jax.ShapeDtypeStruct((r, q_lora), BF16),
                   jax.ShapeDtypeStruct((r, kv_lora), BF16),
                   jax.ShapeDtypeStruct((r, LANE), BF16)],
        compiler_params=_params("arbitrary"),
        name="mla_prep",
    )(down, q_norm, kv_norm, cos, sin)


def _softmax_attend(q, k, v):
    s = lax.dot_general(q, k, (((1,), (1,)), ((), ())), preferred_element_type=F32)
    p = jnp.exp2(s - jnp.max(s, axis=1, keepdims=True))
    o = jnp.dot(p.astype(BF16), v, preferred_element_type=F32)
    return o / jnp.sum(p, axis=1, keepdims=True)


def _attn_lat_kernel(q_ref, knl_ref, krl_ref, vl_ref, knc_ref, krc_ref, vc_ref, o_ref, kcat_ref, vcat_ref):
    ctx_len = knc_ref.shape[0]

    @pl.when(pl.program_id(2) == 0)
    def _():
        kcat_ref[:ctx_len, :LANE] = knc_ref[...]
        kcat_ref[:ctx_len, LANE:] = krc_ref[...]
        kcat_ref[ctx_len:, :LANE] = knl_ref[...]
        kcat_ref[ctx_len:, LANE:] = krl_ref[...]
        vcat_ref[:ctx_len, :] = vc_ref[...]
        vcat_ref[ctx_len:, :] = vl_ref[...]

    k, v = kcat_ref[...], vcat_ref[...]
    half = q_ref.shape[0] // 2
    o_ref[:half, :] = _softmax_attend(q_ref[:half, :], k, v).astype(o_ref.dtype)
    o_ref[half:, :] = _softmax_attend(q_ref[half:, :], k, v).astype(o_ref.dtype)


def _attn_ctx_kernel(q_ref, kn_ref, kr_ref, v_ref, prev_ref, o_ref):
    del prev_ref
    k = jnp.concatenate([kn_ref[...], kr_ref[...]], axis=1)
    o_ref[...] = _softmax_attend(q_ref[...], k, v_ref[...]).astype(o_ref.dtype)


def mla_attention(q, kv, kr, *, lay, heads, with_ctx):
    seq = lay.seq_tiles * ROW_TILE
    tq = 2 * ROW_TILE
    q_tiles = seq // tq
    out_rows = lay.rows if with_ctx else lay.lat_rows

    def lat(col):
        return pl.BlockSpec((seq, LANE), lambda b, h, j: (b, col(h)))

    def ctx3(col):
        return pl.BlockSpec((ROW_TILE, LANE), lambda b, h, j: (lay.ctx_tile(b), col(h)))

    k_col, v_col, r_col = (lambda h: 2 * h), (lambda h: 2 * h + 1), (lambda h: 0)
    att = pl.pallas_call(
        _attn_lat_kernel,
        grid=(lay.batch, heads, q_tiles),
        in_specs=[pl.BlockSpec((tq, MLA_QW), lambda b, h, j: (b * q_tiles + j, h)),
                  lat(k_col), lat(r_col), lat(v_col), ctx3(k_col), ctx3(r_col), ctx3(v_col)],
        out_specs=pl.BlockSpec((tq, MLA_V), lambda b, h, j: (b * q_tiles + j, h)),
        out_shape=jax.ShapeDtypeStruct((out_rows, heads * MLA_V), BF16),
        scratch_shapes=[pltpu.VMEM((ROW_TILE + seq, MLA_QW), BF16), pltpu.VMEM((ROW_TILE + seq, MLA_V), BF16)],
        compiler_params=_params("arbitrary", "arbitrary", "arbitrary"),
        name="mla_attention",
    )(q, kv, kr, kv, kv, kr, kv)
    if not with_ctx:
        return att

    def ctx2(width, col):
        return pl.BlockSpec((ROW_TILE, width), lambda b, h: (lay.ctx_tile(b), col(h)))

    return pl.pallas_call(
        _attn_ctx_kernel,
        grid=(lay.batch, heads),
        in_specs=[ctx2(MLA_QW, lambda h: h), ctx2(LANE, k_col), ctx2(LANE, r_col), ctx2(LANE, v_col),
                  pl.BlockSpec(memory_space=pl.ANY)],
        out_specs=ctx2(MLA_V, lambda h: h),
        out_shape=jax.ShapeDtypeStruct(att.shape, att.dtype),
        input_output_aliases={4: 0},
        compiler_params=_params("arbitrary", "arbitrary"),
        name="mla_attention_ctx",
    )(q, kv, kr, kv, att)


def _conv_gate_kernel(ug_ref, uv_ref, pg_ref, pv_ref, ng_ref, nv_ref, wg_ref, wv_ref, bg_ref, bv_ref, o_ref, *, lay):
    t = pl.program_id(0)
    first, last = lay.seq_first(t), lay.seq_last(t)
    rows = ug_ref.shape[0]
    ridx = lax.broadcasted_iota(jnp.int32, ug_ref.shape, 0)

    def conv(u_ref, p_ref, n_ref, w_ref, b_ref):
        u = u_ref[...]
        prev_row = jnp.where(first, 0.0, p_ref[SUBLANE - 1:SUBLANE, :])
        next_row = jnp.where(last, 0.0, n_ref[0:1, :])
        up = jnp.where(ridx == 0, prev_row, pltpu.roll(u, 1, 0))
        un = jnp.where(ridx == rows - 1, next_row, pltpu.roll(u, rows - 1, 0))
        return up * w_ref[0:1, :] + u * w_ref[1:2, :] + un * w_ref[2:3, :] + b_ref[...]

    gate = conv(ug_ref, pg_ref, ng_ref, wg_ref, bg_ref)
    val = conv(uv_ref, pv_ref, nv_ref, wv_ref, bv_ref)
    o_ref[...] = (gate * jax.nn.sigmoid(gate) * val).astype(o_ref.dtype)


def conv_gate(u, conv_w, conv_b, *, lay, tn=1024):
    r, two_f = u.shape
    f = two_f // 2
    nb = f // tn
    per_tile = ROW_TILE // SUBLANE
    last_sub = r // SUBLANE - 1

    def main(off):
        return pl.BlockSpec((ROW_TILE, tn), lambda t, j: (t, off + j))

    def prev(off):
        return pl.BlockSpec((SUBLANE, tn), lambda t, j: (jnp.maximum(t * per_tile - 1, 0), off + j))

    def nxt(off):
        return pl.BlockSpec((SUBLANE, tn), lambda t, j: (jnp.minimum((t + 1) * per_tile, last_sub), off + j))

    def wspec(off):
        return pl.BlockSpec((3, tn), lambda t, j: (0, off + j))

    def bspec(off):
        return pl.BlockSpec((1, tn), lambda t, j: (0, off + j))

    return pl.pallas_call(
        functools.partial(_conv_gate_kernel, lay=lay),
        grid=(r // ROW_TILE, nb),
        in_specs=[main(0), main(nb), prev(0), prev(nb), nxt(0), nxt(nb), wspec(0), wspec(nb), bspec(0), bspec(nb)],
        out_specs=pl.BlockSpec((ROW_TILE, tn), lambda t, j: (t, j)),
        out_shape=jax.ShapeDtypeStruct((r, f), BF16),
        compiler_params=_params("arbitrary", "arbitrary"),
        name="conv_gate",
    )(u, u, u, u, u, u, conv_w, conv_w, conv_b.reshape(1, two_f), conv_b.reshape(1, two_f))


def _rope_tables(batch, ctx_len, seq):
    per_axis = MLA_ROPE // 2
    inv_freq = ROPE_THETA ** (-jnp.arange(0, per_axis, 2, dtype=F32) / per_axis)
    pos = jnp.arange(seq)
    ang_r = (pos // GRID_W).astype(F32)[:, None] * inv_freq
    ang_c = (pos % GRID_W).astype(F32)[:, None] * inv_freq
    cos64 = jnp.concatenate([jnp.cos(ang_r)] * 2 + [jnp.cos(ang_c)] * 2, axis=1)
    sin64 = jnp.concatenate([-jnp.sin(ang_r), jnp.sin(ang_r), -jnp.sin(ang_c), jnp.sin(ang_c)], axis=1)
    cos_lat = jnp.tile(jnp.concatenate([cos64, cos64], axis=1), (batch, 1))
    sin_lat = jnp.tile(jnp.concatenate([sin64, sin64], axis=1), (batch, 1))
    cos = jnp.concatenate([cos_lat, jnp.ones((batch * ctx_len, LANE), F32)], axis=0)
    sin = jnp.concatenate([sin_lat, jnp.zeros((batch * ctx_len, LANE), F32)], axis=0)
    return cos, sin


def kernel(x, c, ctx, c_ctx, mod_w, mod_b, norm_w, ml_w_in, ml_b_gate, ml_norm, ml_w_out, mla_w_down, mla_q_norm,
           mla_w_uq, mla_kv_norm, mla_w_ukv, mla_w_out, ffn_w_up, ffn_conv_w, ffn_conv_b, ffn_w_down):
    batch, seq, d = x.shape
    ctx_len = ctx.shape[1]
    depth = mod_w.shape[0]
    lay = Layout(batch, seq, ctx_len)
    ml_qk = d // 2
    dk, dv = ml_qk // ML_HEADS, d // ML_HEADS
    heads = d // MLA_NOPE
    q_lora, kv_lora = d // 4, d // 8

    h = jnp.concatenate([x.reshape(batch * seq, d), ctx.reshape(batch * ctx_len, d)], axis=0)
    cond = jnp.concatenate([c_ctx[None], c, jnp.zeros((MOD_ROWS - 1 - batch, d), F32)], axis=0)
    mods = adaln_all(cond, mod_w, mod_b).reshape(depth * MOD_ROWS * N_MOD, 1, d)
    gains = norm_w.reshape(depth * 4, 1, d)
    cos, sin = _rope_tables(batch, ctx_len, seq)
    norm = functools.partial(norm_modulate, mods=mods, norm_w=gains, lay=lay)
    w_in_t = jnp.swapaxes(ml_w_in, 1, 2)

    f = None
    for i in range(depth):
        j = i // 2
        ctx_out = i < depth - 1
        if f is None:
            a = norm(h, layer=i, rows=lay.rows, pre_idx=0, shift_idx=0, scale_idx=1)[0]
        else:
            h, a = norm(h, layer=i, rows=lay.rows, y=f, y_layer=i - 1, gate_idx=5, post_idx=3,
                        pre_idx=0, shift_idx=0, scale_idx=1)
        if i % 2 == 0:
            qkvo = matmul(a, w_in_t, layer=j, w_t=True, n_out=2 * ml_qk + 2 * d)
            gates = matmul(a, w_in_t, layer=j, w_t=True, n_out=4 * ML_HEADS, col0=2 * ml_qk + 2 * d)
            bias = ml_b_gate[j].reshape(1, 4 * ML_HEADS)
            h_f, h_b = mlstm_scan(qkvo, gates, bias, lay=lay, dk=dk, dv=dv)
            mixed = mlstm_readout(h_f, h_b, qkvo, ml_norm[j].reshape(1, d), dv=dv)
        else:
            down_w = mla_w_down[j]
            down = matmul(a, jnp.pad(down_w, ((0, 0), (0, -down_w.shape[1] % 512))))
            cq, ckv, kr = mla_prep(down, mla_q_norm[j].reshape(1, q_lora), mla_kv_norm[j].reshape(1, kv_lora),
                                   cos, sin, q_lora=q_lora, kv_lora=kv_lora)
            w_q = jnp.pad(mla_w_uq[j].reshape(q_lora, heads, MLA_NOPE + MLA_ROPE),
                          ((0, 0), (0, 0), (0, MLA_QW - MLA_NOPE - MLA_ROPE))).reshape(q_lora, heads * MLA_QW)
            q = matmul(cq, w_q, out_dtype=BF16, rope=(cos, sin), rope_scale=MLA_SCALE * math.log2(math.e))
            kv = matmul(ckv, mla_w_ukv, layer=j, out_dtype=BF16)
            mixed = mla_attention(q, kv, kr, lay=lay, heads=heads, with_ctx=ctx_out)
        rows = mixed.shape[0]
        y = matmul(mixed, ml_w_out if i % 2 == 0 else mla_w_out, layer=j)
        h, a = norm(h, layer=i, rows=rows, y=y, gate_idx=2, post_idx=1, pre_idx=2, shift_idx=3, scale_idx=4)
        u = matmul(a, ffn_w_up, layer=i)
        g = conv_gate(u, ffn_conv_w[i], ffn_conv_b[i], lay=lay)
        f = matmul(g, ffn_w_down, layer=i)
    out = norm(h, layer=depth - 1, rows=lay.lat_rows, y=f, gate_idx=5, post_idx=3)[0]
    return out.reshape(batch, seq, d)
```

```python
import functools
import math

import jax
import jax.numpy as jnp
from jax import lax
from jax.experimental import pallas as pl
from jax.experimental.pallas import tpu as pltpu

F32 = jnp.float32
BF16 = jnp.bfloat16

LANE = 128
SUBLANE = 8
VMEM_LIMIT_BYTES = 56 * 1024 * 1024
W_BLOCK_BYTES = 8 * 1024 * 1024

EPS = 1e-6
ML_HEADS = 8
GATE_CAP = 15.0
GRID_W = 64
ROPE_THETA = 10000.0
MLA_NOPE = 128
MLA_ROPE = 64
MLA_V = 128
MLA_SCALE = (MLA_NOPE + MLA_ROPE) ** -0.5
MLA_QW = 2 * LANE
N_MOD = 6
MOD_ROWS = 8

ROW_TILE = 256
ATTN_Q_TILE = 1024
ATTN_SUB_TILE = 256


def _params(*sem):
    return pltpu.CompilerParams(dimension_semantics=sem, vmem_limit_bytes=VMEM_LIMIT_BYTES)


class Layout:
    def __init__(self, batch, seq, ctx_len):
        assert ctx_len == ROW_TILE and seq % (2 * ROW_TILE) == 0 and batch + 1 <= MOD_ROWS
        self.batch = batch
        self.seq_tiles = seq // ROW_TILE
        self.lat_tiles = batch * self.seq_tiles
        self.tiles = self.lat_tiles + batch
        self.lat_rows = batch * seq
        self.rows = self.tiles * ROW_TILE

    def mod_row(self, t):
        return jnp.where(t >= self.lat_tiles, 0, 1 + t // self.seq_tiles)

    def seq_first(self, t):
        return jnp.logical_or(t >= self.lat_tiles, t % self.seq_tiles == 0)

    def seq_last(self, t):
        return jnp.logical_or(t >= self.lat_tiles, t % self.seq_tiles == self.seq_tiles - 1)

    def ctx_tile(self, b):
        return self.lat_tiles + b


def _adaln_kernel(c_ref, w_ref, b_ref, o_ref):
    c = c_ref[...]
    s = c * jax.nn.sigmoid(c)
    o_ref[0] = jnp.dot(s, w_ref[0], preferred_element_type=F32) + b_ref[0]


def adaln_all(cond, mod_w, mod_b, tn=1024):
    depth, d, n = mod_w.shape
    return pl.pallas_call(
        _adaln_kernel,
        grid=(depth, n // tn),
        in_specs=[
            pl.BlockSpec((MOD_ROWS, d), lambda l, j: (0, 0)),
            pl.BlockSpec((1, d, tn), lambda l, j: (l, 0, j)),
            pl.BlockSpec((1, 1, tn), lambda l, j: (l, 0, j)),
        ],
        out_specs=pl.BlockSpec((1, MOD_ROWS, tn), lambda l, j: (l, 0, j)),
        out_shape=jax.ShapeDtypeStruct((depth, MOD_ROWS, n), F32),
        compiler_params=_params("arbitrary", "arbitrary"),
        name="adaln",
    )(cond, mod_w, mod_b.reshape(depth, 1, n))


def _rms(x, g):
    return x * lax.rsqrt(jnp.mean(x * x, axis=-1, keepdims=True) + EPS) * g


def _norm_kernel(*refs, has_y, has_a):
    it = iter(refs)
    h_ref = next(it)
    if has_y:
        y_ref, gate_ref, gpost_ref = next(it), next(it), next(it)
    if has_a:
        gpre_ref, shift_ref, scale_ref = next(it), next(it), next(it)
    if has_y:
        hout_ref = next(it)
    if has_a:
        a_ref = next(it)
    h = h_ref[...]
    if has_y:
        h = h + gate_ref[0] * _rms(y_ref[...], gpost_ref[0])
        hout_ref[...] = h
    if has_a:
        a = _rms(h, gpre_ref[0]) * (1.0 + scale_ref[0]) + shift_ref[0]
        a_ref[...] = a.astype(a_ref.dtype)


def norm_modulate(h, mods, norm_w, layer, *, lay, rows, y=None, y_layer=None, gate_idx=None, post_idx=None,
                  pre_idx=None, shift_idx=None, scale_idx=None):
    d = h.shape[1]
    has_y, has_a = y is not None, pre_idx is not None
    y_layer = layer if y_layer is None else y_layer
    row_spec = pl.BlockSpec((ROW_TILE, d), lambda t: (t, 0))

    def mod_spec(lyr, which):
        return pl.BlockSpec((1, 1, d), lambda t: ((lyr * MOD_ROWS + lay.mod_row(t)) * N_MOD + which, 0, 0))

    def gain_spec(lyr, which):
        return pl.BlockSpec((1, 1, d), lambda t: (lyr * 4 + which, 0, 0))

    args, in_specs = [h], [row_spec]
    if has_y:
        args += [y, mods, norm_w]
        in_specs += [row_spec, mod_spec(y_layer, gate_idx), gain_spec(y_layer, post_idx)]
    if has_a:
        args += [norm_w, mods, mods]
        in_specs += [gain_spec(layer, pre_idx), mod_spec(layer, shift_idx), mod_spec(layer, scale_idx)]
    out_shape, out_specs = [], []
    if has_y:
        out_shape.append(jax.ShapeDtypeStruct((rows, d), F32))
        out_specs.append(row_spec)
    if has_a:
        out_shape.append(jax.ShapeDtypeStruct((rows, d), BF16))
        out_specs.append(row_spec)
    return pl.pallas_call(
        functools.partial(_norm_kernel, has_y=has_y, has_a=has_a),
        grid=(rows // ROW_TILE,),
        in_specs=in_specs,
        out_specs=out_specs,
        out_shape=out_shape,
        compiler_params=_params("arbitrary"),
        name="norm_modulate",
    )(*args)


def _rope_lanes(x, cos, sin):
    lane = lax.broadcasted_iota(jnp.int32, x.shape, 1)
    partner = jnp.where(lane % 32 < 16, pltpu.roll(x, LANE - 16, 1), pltpu.roll(x, 16, 1))
    return x * cos + partner * sin


def _mm_kernel(*refs, rope_scale, w_t):
    if rope_scale is not None:
        a_ref, w_ref, cos_ref, sin_ref, o_ref = refs
    else:
        a_ref, w_ref, o_ref = refs
    a, w = a_ref[...].astype(BF16), w_ref[...].astype(BF16)
    if w_t:
        acc = lax.dot_general(a, w, (((1,), (1,)), ((), ())), preferred_element_type=F32)
    else:
        acc = jnp.dot(a, w, preferred_element_type=F32)
    if rope_scale is not None:
        cos, sin = cos_ref[...], sin_ref[...]
        for g in range(acc.shape[1] // LANE):
            blk = acc[:, g * LANE:(g + 1) * LANE]
            if g % 2 == 1:
                blk = _rope_lanes(blk, cos, sin)
            o_ref[:, g * LANE:(g + 1) * LANE] = (blk * rope_scale).astype(o_ref.dtype)
    else:
        o_ref[...] = acc.astype(o_ref.dtype)


def _pick_tm(r, cap):
    return max(tm for tm in range(16, cap + 1, 16) if r % tm == 0)


def _pick_tn(n, k):
    tn = 256
    while n % (2 * tn) == 0 and 2 * tn * k * 4 <= W_BLOCK_BYTES and 2 * tn <= 2048:
        tn *= 2
    return min(tn, n)


def matmul(a, w, *, layer=None, n_out=None, w_t=False, col0=0, tm_cap=1088, out_dtype=F32, rope=None,
           rope_scale=None):
    r, k = a.shape
    n_w, k_w = (w.shape[-2], w.shape[-1]) if w_t else (w.shape[-1], w.shape[-2])
    n_out = n_w if n_out is None else n_out
    tn = _pick_tn(n_out, k)
    assert n_out % tn == 0 and col0 % tn == 0 and k_w == k
    cb0 = col0 // tn
    tm = _pick_tm(r, tm_cap)
    w_block = (tn, k) if w_t else (k, tn)
    w_index = (lambda j: (cb0 + j, 0)) if w_t else (lambda j: (0, cb0 + j))
    if layer is None:
        w_spec = pl.BlockSpec(w_block, lambda i, j: w_index(j))
    else:
        w_spec = pl.BlockSpec((None,) + w_block, lambda i, j: (layer,) + w_index(j))
    args = [a, w]
    in_specs = [pl.BlockSpec((tm, k), lambda i, j: (i, 0)), w_spec]
    if rope is not None:
        args += list(rope)
        in_specs += [pl.BlockSpec((tm, LANE), lambda i, j: (i, 0))] * 2
    return pl.pallas_call(
        functools.partial(_mm_kernel, rope_scale=rope_scale if rope is not None else None, w_t=w_t),
        grid=(r // tm, n_out // tn),
        in_specs=in_specs,
        out_specs=pl.BlockSpec((tm, tn), lambda i, j: (i, j)),
        out_shape=jax.ShapeDtypeStruct((r, n_out), out_dtype),
        compiler_params=_params("arbitrary", "arbitrary"),
        name="matmul",
    )(*args)


def _col_to_row(col, eye):
    return jnp.sum(jnp.where(eye, col, 0.0), axis=0, keepdims=True)


def _mlstm_chunk(q, k, v, ig, lf, mask, eye, c_ref, n_ref, m_ref):
    neg_inf = -jnp.inf
    lf_row = _col_to_row(lf, eye)
    b = jnp.sum(jnp.where(mask, lf_row, 0.0), axis=1, keepdims=True)
    b_end = jnp.sum(lf, axis=0, keepdims=True)
    src = ig - b
    dmat = jnp.where(mask, b + _col_to_row(src, eye), neg_inf)
    m_prev = m_ref[...]
    inter = b + m_prev
    m_out = jnp.maximum(inter, jnp.max(dmat, axis=1, keepdims=True))
    w_intra = jnp.exp(dmat - m_out)
    w_inter = jnp.exp(inter - m_out)
    qk = lax.dot_general(q, k, (((1,), (1,)), ((), ())), preferred_element_type=F32)
    s = qk * w_intra
    c_prev = c_ref[...]
    num = jnp.dot(s.astype(BF16), v, preferred_element_type=F32)
    num = num + w_inter * jnp.dot(q, c_prev.astype(BF16), preferred_element_type=F32)
    qn = jnp.sum(q.astype(F32) * n_ref[...], axis=1, keepdims=True)
    den = jnp.sum(s, axis=1, keepdims=True) + w_inter * qn
    den = jnp.maximum(jnp.abs(den), jnp.exp(-m_out))
    h = num / den
    g = b_end + src
    m_new = jnp.maximum(b_end + m_prev, jnp.max(g, axis=0, keepdims=True))
    wg = jnp.exp(g - m_new)
    decay = jnp.exp(b_end + m_prev - m_new)
    kw = k.astype(F32) * wg
    c_ref[...] = decay * c_prev + lax.dot_general(
        kw.astype(BF16), v, (((0,), (0,)), ((), ())), preferred_element_type=F32)
    n_ref[...] = decay * n_ref[...] + jnp.sum(kw, axis=0, keepdims=True)
    m_ref[...] = m_new
    return h


def _mlstm_kernel(qf_ref, kf_ref, vf_ref, gf_ref, qb_ref, kb_ref, vb_ref, gb_ref, bias_ref,
                  hf_ref, hb_ref, cf_ref, nf_ref, mf_ref, cb_ref, nb_ref, mb_ref, *, dk_scale):
    head = pl.program_id(1)

    @pl.when(pl.program_id(2) == 0)
    def _():
        for ref in (cf_ref, nf_ref, mf_ref, cb_ref, nb_ref, mb_ref):
            ref[...] = jnp.zeros_like(ref)

    chunk = qf_ref.shape[0]
    row = lax.broadcasted_iota(jnp.int32, (chunk, chunk), 0)
    col = lax.broadcasted_iota(jnp.int32, (chunk, chunk), 1)
    eye = row == col
    lane = lax.broadcasted_iota(jnp.int32, gf_ref.shape, 1)

    def gate_cols(g_ref, first):
        g = g_ref[...] + bias_ref[...]
        g = GATE_CAP * jnp.tanh(g / GATE_CAP)
        ig = jnp.sum(jnp.where(lane == first * ML_HEADS + head, g, 0.0), axis=1, keepdims=True)
        fg = jnp.sum(jnp.where(lane == (first + 1) * ML_HEADS + head, g, 0.0), axis=1, keepdims=True)
        return ig, jax.nn.log_sigmoid(fg)

    def run(q_ref, k_ref, v_ref, g_ref, first, mask, c_ref, n_ref, m_ref, h_ref):
        ig, lf = gate_cols(g_ref, first)
        q = q_ref[...] * dk_scale
        h = _mlstm_chunk(q, k_ref[...], v_ref[...], ig, lf, mask, eye, c_ref, n_ref, m_ref)
        h_ref[...] = h

    run(qf_ref, kf_ref, vf_ref, gf_ref, 0, col <= row, cf_ref, nf_ref, mf_ref, hf_ref)
    run(qb_ref, kb_ref, vb_ref, gb_ref, 2, col >= row, cb_ref, nb_ref, mb_ref, hb_ref)


def mlstm_scan(qkv, gates, gate_bias, *, lay, dk, dv):
    r = qkv.shape[0]
    n_gates = gates.shape[1]
    st = lay.seq_tiles

    def fwd_tile(b, c):
        return jnp.where(c == 0, lay.ctx_tile(b), b * st + c - 1)

    def bwd_tile(b, c):
        return jnp.where(c == 0, lay.ctx_tile(b), b * st + st - c)

    def specs(tile):
        return [
            pl.BlockSpec((ROW_TILE, dk), lambda b, h, c: (tile(b, c), h)),
            pl.BlockSpec((ROW_TILE, dk), lambda b, h, c: (tile(b, c), ML_HEADS + h)),
            pl.BlockSpec((ROW_TILE, dv), lambda b, h, c: (tile(b, c), (2 * ML_HEADS * dk) // dv + h)),
            pl.BlockSpec((ROW_TILE, n_gates), lambda b, h, c: (tile(b, c), 0)),
        ]

    def out_spec(tile):
        return pl.BlockSpec((ROW_TILE, dv), lambda b, h, c: (tile(b, c), h))

    state = [pltpu.VMEM((dk, dv), F32), pltpu.VMEM((1, dk), F32), pltpu.VMEM((1, 1), F32)]
    return pl.pallas_call(
        functools.partial(_mlstm_kernel, dk_scale=dk ** -0.5),
        grid=(lay.batch, ML_HEADS, st + 1),
        in_specs=specs(fwd_tile) + specs(bwd_tile) + [pl.BlockSpec((1, n_gates), lambda b, h, c: (0, 0))],
        out_specs=[out_spec(fwd_tile), out_spec(bwd_tile)],
        out_shape=[jax.ShapeDtypeStruct((r, ML_HEADS * dv), F32)] * 2,
        scratch_shapes=state + state,
        compiler_params=_params("arbitrary", "arbitrary", "arbitrary"),
        name="mlstm_scan",
    )(qkv, qkv, qkv, gates, qkv, qkv, qkv, gates, gate_bias)


def _readout_kernel(hf_ref, hb_ref, o_ref, g_ref, out_ref, *, dv):
    for hd in range(ML_HEADS):
        sl = slice(hd * dv, (hd + 1) * dv)
        hs = hf_ref[:, sl] + hb_ref[:, sl]
        o = o_ref[:, sl]
        out_ref[:, sl] = (jax.nn.sigmoid(o) * _rms(hs, g_ref[:, sl])).astype(out_ref.dtype)


def mlstm_readout(h_f, h_b, o, g_norm, *, dv):
    r, width = h_f.shape
    row = pl.BlockSpec((ROW_TILE, width), lambda t: (t, 0))
    return pl.pallas_call(
        functools.partial(_readout_kernel, dv=dv),
        grid=(r // ROW_TILE,),
        in_specs=[row, row, row, pl.BlockSpec((1, width), lambda t: (0, 0))],
        out_specs=row,
        out_shape=jax.ShapeDtypeStruct((r, width), BF16),
        compiler_params=_params("arbitrary"),
        name="mlstm_readout",
    )(h_f, h_b, o, g_norm)


def _mla_prep_kernel(down_ref, qn_ref, kvn_ref, cos_ref, sin_ref, cq_ref, ckv_ref, kr_ref, *, q_lora, kv_lora):
    cq_ref[...] = _rms(down_ref[:, :q_lora], qn_ref[...]).astype(cq_ref.dtype)
    ckv_ref[...] = _rms(down_ref[:, q_lora:q_lora + kv_lora], kvn_ref[...]).astype(ckv_ref.dtype)
    kr = down_ref[:, q_lora + kv_lora:q_lora + kv_lora + LANE]
    lane = lax.broadcasted_iota(jnp.int32, kr.shape, 1)
    kr = jnp.where(lane < MLA_ROPE, kr, 0.0)
    kr_ref[...] = _rope_lanes(kr, cos_ref[...], sin_ref[...]).astype(kr_ref.dtype)


def mla_prep(down, q_norm, kv_norm, cos, sin, *, q_lora, kv_lora):
    r, width = down.shape
    return pl.pallas_call(
        functools.partial(_mla_prep_kernel, q_lora=q_lora, kv_lora=kv_lora),
        grid=(r // ROW_TILE,),
        in_specs=[pl.BlockSpec((ROW_TILE, width), lambda t: (t, 0)),
                  pl.BlockSpec((1, q_lora), lambda t: (0, 0)),
                  pl.BlockSpec((1, kv_lora), lambda t: (0, 0)),
                  pl.BlockSpec((ROW_TILE, LANE), lambda t: (t, 0)),
                  pl.BlockSpec((ROW_TILE, LANE), lambda t: (t, 0))],
        out_specs=[pl.BlockSpec((ROW_TILE, q_lora), lambda t: (t, 0)),
                   pl.BlockSpec((ROW_TILE, kv_lora), lambda t: (t, 0)),
                   pl.BlockSpec((ROW_TILE, LANE), lambda t: (t, 0))],
        out_shape=[jax.ShapeDtypeStruct((r, q_lora), BF16),
                   jax.ShapeDtypeStruct((r, kv_lora), BF16),
                   jax.ShapeDtypeStruct((r, LANE), BF16)],
        compiler_params=_params("arbitrary"),
        name="mla_prep",
    )(down, q_norm, kv_norm, cos, sin)


def _softmax_attend(q, k_t, v):
    s = jnp.dot(q, k_t, preferred_element_type=F32)
    p = jnp.exp2(s - jnp.max(s, axis=1, keepdims=True))
    o = jnp.dot(p.astype(BF16), v, preferred_element_type=F32)
    return o / jnp.sum(p, axis=1, keepdims=True)


def _attn_lat_kernel(q_ref, knl_ref, krl_ref, vl_ref, knc_ref, krc_ref, vc_ref, o_ref, kcat_ref, vcat_ref):
    ctx_len = vc_ref.shape[0]

    @pl.when(pl.program_id(2) == 0)
    def _():
        kcat_ref[:MLA_NOPE, :ctx_len] = knc_ref[...]
        kcat_ref[MLA_NOPE:, :ctx_len] = krc_ref[...]
        kcat_ref[:MLA_NOPE, ctx_len:] = knl_ref[...]
        kcat_ref[MLA_NOPE:, ctx_len:] = krl_ref[...]
        vcat_ref[:ctx_len, :] = vc_ref[...]
        vcat_ref[ctx_len:, :] = vl_ref[...]

    k_t, v = kcat_ref[...], vcat_ref[...]
    sub = min(ATTN_SUB_TILE, q_ref.shape[0])
    for part in range(q_ref.shape[0] // sub):
        rows = slice(part * sub, (part + 1) * sub)
        o_ref[rows, :] = _softmax_attend(q_ref[rows, :], k_t, v).astype(o_ref.dtype)


def _attn_ctx_kernel(q_ref, kn_ref, kr_ref, v_ref, prev_ref, o_ref):
    del prev_ref
    k_t = jnp.concatenate([kn_ref[...], kr_ref[...]], axis=0)
    o_ref[...] = _softmax_attend(q_ref[...], k_t, v_ref[...]).astype(o_ref.dtype)


def mla_attention(q, k_t, kr_t, v, *, lay, heads, with_ctx):
    seq = lay.seq_tiles * ROW_TILE
    tq = min(ATTN_Q_TILE, seq)
    q_tiles = seq // tq
    out_rows = lay.rows if with_ctx else lay.lat_rows

    def keys_lat(row):
        return pl.BlockSpec((MLA_NOPE, seq), lambda b, h, j: (row(h), b))

    def keys_ctx(row):
        return pl.BlockSpec((MLA_NOPE, ROW_TILE), lambda b, h, j: (row(h), lay.ctx_tile(b)))

    own, shared = (lambda h: h), (lambda h: 0)
    att = pl.pallas_call(
        _attn_lat_kernel,
        grid=(lay.batch, heads, q_tiles),
        in_specs=[pl.BlockSpec((tq, MLA_QW), lambda b, h, j: (b * q_tiles + j, h)),
                  keys_lat(own), keys_lat(shared), pl.BlockSpec((seq, MLA_V), lambda b, h, j: (b, h)),
                  keys_ctx(own), keys_ctx(shared),
                  pl.BlockSpec((ROW_TILE, MLA_V), lambda b, h, j: (lay.ctx_tile(b), h))],
        out_specs=pl.BlockSpec((tq, MLA_V), lambda b, h, j: (b * q_tiles + j, h)),
        out_shape=jax.ShapeDtypeStruct((out_rows, heads * MLA_V), BF16),
        scratch_shapes=[pltpu.VMEM((MLA_QW, ROW_TILE + seq), BF16), pltpu.VMEM((ROW_TILE + seq, MLA_V), BF16)],
        compiler_params=_params("arbitrary", "arbitrary", "arbitrary"),
        name="mla_attention",
    )(q, k_t, kr_t, v, k_t, kr_t, v)
    if not with_ctx:
        return att

    def ctx_keys(row):
        return pl.BlockSpec((MLA_NOPE, ROW_TILE), lambda b, h: (row(h), lay.ctx_tile(b)))

    def ctx_rows(width):
        return pl.BlockSpec((ROW_TILE, width), lambda b, h: (lay.ctx_tile(b), h))

    return pl.pallas_call(
        _attn_ctx_kernel,
        grid=(lay.batch, heads),
        in_specs=[ctx_rows(MLA_QW), ctx_keys(own), ctx_keys(shared), ctx_rows(MLA_V),
                  pl.BlockSpec(memory_space=pl.ANY)],
        out_specs=ctx_rows(MLA_V),
        out_shape=jax.ShapeDtypeStruct(att.shape, att.dtype),
        input_output_aliases={4: 0},
        compiler_params=_params("arbitrary", "arbitrary"),
        name="mla_attention_ctx",
    )(q, k_t, kr_t, v, att)


def _conv_gate_kernel(ug_ref, uv_ref, pg_ref, pv_ref, ng_ref, nv_ref, wg_ref, wv_ref, bg_ref, bv_ref, o_ref, *, lay):
    t = pl.program_id(0)
    first, last = lay.seq_first(t), lay.seq_last(t)
    rows = ug_ref.shape[0]
    ridx = lax.broadcasted_iota(jnp.int32, ug_ref.shape, 0)

    def conv(u_ref, p_ref, n_ref, w_ref, b_ref):
        u = u_ref[...]
        prev_row = jnp.where(first, 0.0, p_ref[SUBLANE - 1:SUBLANE, :])
        next_row = jnp.where(last, 0.0, n_ref[0:1, :])
        up = jnp.where(ridx == 0, prev_row, pltpu.roll(u, 1, 0))
        un = jnp.where(ridx == rows - 1, next_row, pltpu.roll(u, rows - 1, 0))
        return up * w_ref[0:1, :] + u * w_ref[1:2, :] + un * w_ref[2:3, :] + b_ref[...]

    gate = conv(ug_ref, pg_ref, ng_ref, wg_ref, bg_ref)
    val = conv(uv_ref, pv_ref, nv_ref, wv_ref, bv_ref)
    o_ref[...] = (gate * jax.nn.sigmoid(gate) * val).astype(o_ref.dtype)


def conv_gate(u, conv_w, conv_b, *, lay, tn=1024):
    r, two_f = u.shape
    f = two_f // 2
    nb = f // tn
    per_tile = ROW_TILE // SUBLANE
    last_sub = r // SUBLANE - 1

    def main(off):
        return pl.BlockSpec((ROW_TILE, tn), lambda t, j: (t, off + j))

    def prev(off):
        return pl.BlockSpec((SUBLANE, tn), lambda t, j: (jnp.maximum(t * per_tile - 1, 0), off + j))

    def nxt(off):
        return pl.BlockSpec((SUBLANE, tn), lambda t, j: (jnp.minimum((t + 1) * per_tile, last_sub), off + j))

    def wspec(off):
        return pl.BlockSpec((3, tn), lambda t, j: (0, off + j))

    def bspec(off):
        return pl.BlockSpec((1, tn), lambda t, j: (0, off + j))

    return pl.pallas_call(
        functools.partial(_conv_gate_kernel, lay=lay),
        grid=(r // ROW_TILE, nb),
        in_specs=[main(0), main(nb), prev(0), prev(nb), nxt(0), nxt(nb), wspec(0), wspec(nb), bspec(0), bspec(nb)],
        out_specs=pl.BlockSpec((ROW_TILE, tn), lambda t, j: (t, j)),
        out_shape=jax.ShapeDtypeStruct((r, f), BF16),
        compiler_params=_params("arbitrary", "arbitrary"),
        name="conv_gate",
    )(u, u, u, u, u, u, conv_w, conv_w, conv_b.reshape(1, two_f), conv_b.reshape(1, two_f))


def _rope_tables(batch, ctx_len, seq):
    per_axis = MLA_ROPE // 2
    inv_freq = ROPE_THETA ** (-jnp.arange(0, per_axis, 2, dtype=F32) / per_axis)
    pos = jnp.arange(seq)
    ang_r = (pos // GRID_W).astype(F32)[:, None] * inv_freq
    ang_c = (pos % GRID_W).astype(F32)[:, None] * inv_freq
    cos64 = jnp.concatenate([jnp.cos(ang_r)] * 2 + [jnp.cos(ang_c)] * 2, axis=1)
    sin64 = jnp.concatenate([-jnp.sin(ang_r), jnp.sin(ang_r), -jnp.sin(ang_c), jnp.sin(ang_c)], axis=1)
    cos_lat = jnp.tile(jnp.concatenate([cos64, cos64], axis=1), (batch, 1))
    sin_lat = jnp.tile(jnp.concatenate([sin64, sin64], axis=1), (batch, 1))
    cos = jnp.concatenate([cos_lat, jnp.ones((batch * ctx_len, LANE), F32)], axis=0)
    sin = jnp.concatenate([sin_lat, jnp.zeros((batch * ctx_len, LANE), F32)], axis=0)
    return cos, sin


def kernel(x, c, ctx, c_ctx, mod_w, mod_b, norm_w, ml_w_in, ml_b_gate, ml_norm, ml_w_out, mla_w_down, mla_q_norm,
           mla_w_uq, mla_kv_norm, mla_w_ukv, mla_w_out, ffn_w_up, ffn_conv_w, ffn_conv_b, ffn_w_down):
    batch, seq, d = x.shape
    ctx_len = ctx.shape[1]
    depth = mod_w.shape[0]
    lay = Layout(batch, seq, ctx_len)
    ml_qk = d // 2
    dk, dv = ml_qk // ML_HEADS, d // ML_HEADS
    heads = d // MLA_NOPE
    q_lora, kv_lora = d // 4, d // 8

    h = jnp.concatenate([x.reshape(batch * seq, d), ctx.reshape(batch * ctx_len, d)], axis=0)
    cond = jnp.concatenate([c_ctx[None], c, jnp.zeros((MOD_ROWS - 1 - batch, d), F32)], axis=0)
    mods = adaln_all(cond, mod_w, mod_b).reshape(depth * MOD_ROWS * N_MOD, 1, d)
    gains = norm_w.reshape(depth * 4, 1, d)
    cos, sin = _rope_tables(batch, ctx_len, seq)
    norm = functools.partial(norm_modulate, mods=mods, norm_w=gains, lay=lay)
    w_in_t = jnp.swapaxes(ml_w_in, 1, 2)

    f = None
    for i in range(depth):
        j = i // 2
        ctx_out = i < depth - 1
        if f is None:
            a = norm(h, layer=i, rows=lay.rows, pre_idx=0, shift_idx=0, scale_idx=1)[0]
        else:
            h, a = norm(h, layer=i, rows=lay.rows, y=f, y_layer=i - 1, gate_idx=5, post_idx=3,
                        pre_idx=0, shift_idx=0, scale_idx=1)
        if i % 2 == 0:
            qkv = matmul(a, w_in_t, layer=j, w_t=True, n_out=2 * ml_qk + d, out_dtype=BF16)
            o = matmul(a, w_in_t, layer=j, w_t=True, n_out=d, col0=2 * ml_qk + d)
            gates = matmul(a, w_in_t, layer=j, w_t=True, n_out=4 * ML_HEADS, col0=2 * ml_qk + 2 * d)
            bias = ml_b_gate[j].reshape(1, 4 * ML_HEADS)
            h_f, h_b = mlstm_scan(qkv, gates, bias, lay=lay, dk=dk, dv=dv)
            mixed = mlstm_readout(h_f, h_b, o, ml_norm[j].reshape(1, d), dv=dv)
        else:
            down_w = mla_w_down[j]
            down = matmul(a, jnp.pad(down_w, ((0, 0), (0, -down_w.shape[1] % 512))))
            cq, ckv, kr = mla_prep(down, mla_q_norm[j].reshape(1, q_lora), mla_kv_norm[j].reshape(1, kv_lora),
                                   cos, sin, q_lora=q_lora, kv_lora=kv_lora)
            w_q = jnp.pad(mla_w_uq[j].reshape(q_lora, heads, MLA_NOPE + MLA_ROPE),
                          ((0, 0), (0, 0), (0, MLA_QW - MLA_NOPE - MLA_ROPE))).reshape(q_lora, heads * MLA_QW)
            q = matmul(cq, w_q, out_dtype=BF16, rope=(cos, sin), rope_scale=MLA_SCALE * math.log2(math.e))
            w_kv = mla_w_ukv[j].reshape(kv_lora, heads, MLA_NOPE + MLA_V)
            w_k_t = jnp.transpose(w_kv[:, :, :MLA_NOPE].reshape(kv_lora, heads * MLA_NOPE))
            w_v = w_kv[:, :, MLA_NOPE:].reshape(kv_lora, heads * MLA_V)
            k_t = matmul(w_k_t, ckv, w_t=True, out_dtype=BF16)
            v = matmul(ckv, w_v, out_dtype=BF16)
            mixed = mla_attention(q, k_t, jnp.transpose(kr), v, lay=lay, heads=heads, with_ctx=ctx_out)
        rows = mixed.shape[0]
        y = matmul(mixed, ml_w_out if i % 2 == 0 else mla_w_out, layer=j)
        h, a = norm(h, layer=i, rows=rows, y=y, gate_idx=2, post_idx=1, pre_idx=2, shift_idx=3, scale_idx=4)
        u = matmul(a, ffn_w_up, layer=i)
        g = conv_gate(u, ffn_conv_w[i], ffn_conv_b[i], lay=lay)
        f = matmul(g, ffn_w_down, layer=i)
    out = norm(h, layer=depth - 1, rows=lay.lat_rows, y=f, gate_idx=5, post_idx=3)[0]
    return out.reshape(batch, seq, d)
```

```python
import functools
import math

import jax
import jax.numpy as jnp
from jax import lax
from jax.experimental import pallas as pl
from jax.experimental.pallas import tpu as pltpu

F32 = jnp.float32
BF16 = jnp.bfloat16

LANE = 128
SUBLANE = 8
VMEM_LIMIT_BYTES = 56 * 1024 * 1024
W_BLOCK_BYTES = 8 * 1024 * 1024

EPS = 1e-6
ML_HEADS = 8
GATE_CAP = 15.0
GRID_W = 64
ROPE_THETA = 10000.0
MLA_NOPE = 128
MLA_ROPE = 64
MLA_V = 128
MLA_SCALE = (MLA_NOPE + MLA_ROPE) ** -0.5
MLA_QW = 2 * LANE
N_MOD = 6
MOD_ROWS = 8

ROW_TILE = 256
ATTN_Q_TILE = 2048
ATTN_SUB_TILE = 256
SCAN_HEADS_PER_STEP = 2


def _params(*sem):
    return pltpu.CompilerParams(dimension_semantics=sem, vmem_limit_bytes=VMEM_LIMIT_BYTES)


class Layout:
    def __init__(self, batch, seq, ctx_len):
        assert ctx_len == ROW_TILE and seq % (2 * ROW_TILE) == 0 and batch + 1 <= MOD_ROWS
        self.batch = batch
        self.seq_tiles = seq // ROW_TILE
        self.lat_tiles = batch * self.seq_tiles
        self.tiles = self.lat_tiles + batch
        self.lat_rows = batch * seq
        self.rows = self.tiles * ROW_TILE

    def mod_row(self, t):
        return jnp.where(t >= self.lat_tiles, 0, 1 + t // self.seq_tiles)

    def seq_first(self, t):
        return jnp.logical_or(t >= self.lat_tiles, t % self.seq_tiles == 0)

    def seq_last(self, t):
        return jnp.logical_or(t >= self.lat_tiles, t % self.seq_tiles == self.seq_tiles - 1)

    def ctx_tile(self, b):
        return self.lat_tiles + b


def _adaln_kernel(c_ref, w_ref, b_ref, o_ref):
    c = c_ref[...]
    s = c * jax.nn.sigmoid(c)
    o_ref[0] = jnp.dot(s, w_ref[0], preferred_element_type=F32) + b_ref[0]


def adaln_all(cond, mod_w, mod_b, tn=1024):
    depth, d, n = mod_w.shape
    return pl.pallas_call(
        _adaln_kernel,
        grid=(depth, n // tn),
        in_specs=[
            pl.BlockSpec((MOD_ROWS, d), lambda l, j: (0, 0)),
            pl.BlockSpec((1, d, tn), lambda l, j: (l, 0, j)),
            pl.BlockSpec((1, 1, tn), lambda l, j: (l, 0, j)),
        ],
        out_specs=pl.BlockSpec((1, MOD_ROWS, tn), lambda l, j: (l, 0, j)),
        out_shape=jax.ShapeDtypeStruct((depth, MOD_ROWS, n), F32),
        compiler_params=_params("arbitrary", "arbitrary"),
        name="adaln",
    )(cond, mod_w, mod_b.reshape(depth, 1, n))


def _rms(x, g):
    return x * lax.rsqrt(jnp.mean(x * x, axis=-1, keepdims=True) + EPS) * g


def _norm_kernel(*refs, has_y, has_a):
    it = iter(refs)
    h_ref = next(it)
    if has_y:
        y_ref, gate_ref, gpost_ref = next(it), next(it), next(it)
    if has_a:
        gpre_ref, shift_ref, scale_ref = next(it), next(it), next(it)
    if has_y:
        hout_ref = next(it)
    if has_a:
        a_ref = next(it)
    h = h_ref[...]
    if has_y:
        h = h + gate_ref[0] * _rms(y_ref[...], gpost_ref[0])
        hout_ref[...] = h
    if has_a:
        a = _rms(h, gpre_ref[0]) * (1.0 + scale_ref[0]) + shift_ref[0]
        a_ref[...] = a.astype(a_ref.dtype)


def norm_modulate(h, mods, norm_w, layer, *, lay, rows, y=None, y_layer=None, gate_idx=None, post_idx=None,
                  pre_idx=None, shift_idx=None, scale_idx=None):
    d = h.shape[1]
    has_y, has_a = y is not None, pre_idx is not None
    y_layer = layer if y_layer is None else y_layer
    row_spec = pl.BlockSpec((ROW_TILE, d), lambda t: (t, 0))

    def mod_spec(lyr, which):
        return pl.BlockSpec((1, 1, d), lambda t: ((lyr * MOD_ROWS + lay.mod_row(t)) * N_MOD + which, 0, 0))

    def gain_spec(lyr, which):
        return pl.BlockSpec((1, 1, d), lambda t: (lyr * 4 + which, 0, 0))

    args, in_specs = [h], [row_spec]
    if has_y:
        args += [y, mods, norm_w]
        in_specs += [row_spec, mod_spec(y_layer, gate_idx), gain_spec(y_layer, post_idx)]
    if has_a:
        args += [norm_w, mods, mods]
        in_specs += [gain_spec(layer, pre_idx), mod_spec(layer, shift_idx), mod_spec(layer, scale_idx)]
    out_shape, out_specs = [], []
    if has_y:
        out_shape.append(jax.ShapeDtypeStruct((rows, d), F32))
        out_specs.append(row_spec)
    if has_a:
        out_shape.append(jax.ShapeDtypeStruct((rows, d), BF16))
        out_specs.append(row_spec)
    return pl.pallas_call(
        functools.partial(_norm_kernel, has_y=has_y, has_a=has_a),
        grid=(rows // ROW_TILE,),
        in_specs=in_specs,
        out_specs=out_specs,
        out_shape=out_shape,
        compiler_params=_params("arbitrary"),
        name="norm_modulate",
    )(*args)


def _rope_lanes(x, cos, sin):
    lane = lax.broadcasted_iota(jnp.int32, x.shape, 1)
    partner = jnp.where(lane % 32 < 16, pltpu.roll(x, LANE - 16, 1), pltpu.roll(x, 16, 1))
    return x * cos + partner * sin


def _mm_kernel(*refs, rope_scale, w_t):
    if rope_scale is not None:
        a_ref, w_ref, cos_ref, sin_ref, o_ref = refs
    else:
        a_ref, w_ref, o_ref = refs
    a, w = a_ref[...].astype(BF16), w_ref[...].astype(BF16)
    if w_t:
        acc = lax.dot_general(a, w, (((1,), (1,)), ((), ())), preferred_element_type=F32)
    else:
        acc = jnp.dot(a, w, preferred_element_type=F32)
    if rope_scale is not None:
        cos, sin = cos_ref[...], sin_ref[...]
        for g in range(acc.shape[1] // LANE):
            blk = acc[:, g * LANE:(g + 1) * LANE]
            if g % 2 == 1:
                blk = _rope_lanes(blk, cos, sin)
            o_ref[:, g * LANE:(g + 1) * LANE] = (blk * rope_scale).astype(o_ref.dtype)
    else:
        o_ref[...] = acc.astype(o_ref.dtype)


def _pick_tm(r, cap):
    return max(tm for tm in range(16, cap + 1, 16) if r % tm == 0)


def _pick_tn(n, k):
    tn = 256
    while n % (2 * tn) == 0 and 2 * tn * k * 4 <= W_BLOCK_BYTES and 2 * tn <= 2048:
        tn *= 2
    return min(tn, n)


def matmul(a, w, *, layer=None, n_out=None, w_t=False, col0=0, tm_cap=1088, out_dtype=F32, rope=None,
           rope_scale=None):
    r, k = a.shape
    n_w, k_w = (w.shape[-2], w.shape[-1]) if w_t else (w.shape[-1], w.shape[-2])
    n_out = n_w if n_out is None else n_out
    tn = _pick_tn(n_out, k)
    assert n_out % tn == 0 and col0 % tn == 0 and k_w == k
    cb0 = col0 // tn
    tm = _pick_tm(r, tm_cap)
    w_block = (tn, k) if w_t else (k, tn)
    w_index = (lambda j: (cb0 + j, 0)) if w_t else (lambda j: (0, cb0 + j))
    if layer is None:
        w_spec = pl.BlockSpec(w_block, lambda i, j: w_index(j))
    else:
        w_spec = pl.BlockSpec((None,) + w_block, lambda i, j: (layer,) + w_index(j))
    args = [a, w]
    in_specs = [pl.BlockSpec((tm, k), lambda i, j: (i, 0)), w_spec]
    if rope is not None:
        args += list(rope)
        in_specs += [pl.BlockSpec((tm, LANE), lambda i, j: (i, 0))] * 2
    return pl.pallas_call(
        functools.partial(_mm_kernel, rope_scale=rope_scale if rope is not None else None, w_t=w_t),
        grid=(r // tm, n_out // tn),
        in_specs=in_specs,
        out_specs=pl.BlockSpec((tm, tn), lambda i, j: (i, j)),
        out_shape=jax.ShapeDtypeStruct((r, n_out), out_dtype),
        compiler_params=_params("arbitrary", "arbitrary"),
        name="matmul",
    )(*args)


def _col_to_row(col, eye):
    return jnp.sum(jnp.where(eye, col, 0.0), axis=0, keepdims=True)


def _mlstm_chunk(q, k, v, ig, lf, mask, eye, c_ref, n_ref, m_ref):
    neg_inf = -jnp.inf
    lf_row = _col_to_row(lf, eye)
    b = jnp.sum(jnp.where(mask, lf_row, 0.0), axis=1, keepdims=True)
    b_end = jnp.sum(lf, axis=0, keepdims=True)
    src = ig - b
    dmat = jnp.where(mask, b + _col_to_row(src, eye), neg_inf)
    m_prev = m_ref[...]
    inter = b + m_prev
    m_out = jnp.maximum(inter, jnp.max(dmat, axis=1, keepdims=True))
    w_intra = jnp.exp(dmat - m_out)
    w_inter = jnp.exp(inter - m_out)
    qk = lax.dot_general(q, k, (((1,), (1,)), ((), ())), preferred_element_type=F32)
    s = qk * w_intra
    c_prev = c_ref[...]
    num = jnp.dot(s.astype(BF16), v, preferred_element_type=F32)
    num = num + w_inter * jnp.dot(q, c_prev.astype(BF16), preferred_element_type=F32)
    qn = jnp.sum(q.astype(F32) * n_ref[...], axis=1, keepdims=True)
    den = jnp.sum(s, axis=1, keepdims=True) + w_inter * qn
    den = jnp.maximum(jnp.abs(den), jnp.exp(-m_out))
    h = num / den
    g = b_end + src
    m_new = jnp.maximum(b_end + m_prev, jnp.max(g, axis=0, keepdims=True))
    wg = jnp.exp(g - m_new)
    decay = jnp.exp(b_end + m_prev - m_new)
    kw = k.astype(F32) * wg
    c_ref[...] = decay * c_prev + lax.dot_general(
        kw.astype(BF16), v, (((0,), (0,)), ((), ())), preferred_element_type=F32)
    n_ref[...] = decay * n_ref[...] + jnp.sum(kw, axis=0, keepdims=True)
    m_ref[...] = m_new
    return h


def _mlstm_kernel(qf_ref, kf_ref, vf_ref, gf_ref, qb_ref, kb_ref, vb_ref, gb_ref, bias_ref,
                  hf_ref, hb_ref, cf_ref, nf_ref, mf_ref, cb_ref, nb_ref, mb_ref, *, dk_scale):
    heads_per_step = cf_ref.shape[0]
    dk, dv = cf_ref.shape[1], cf_ref.shape[2]

    @pl.when(pl.program_id(2) == 0)
    def _():
        for ref in (cf_ref, nf_ref, mf_ref, cb_ref, nb_ref, mb_ref):
            ref[...] = jnp.zeros_like(ref)

    chunk = qf_ref.shape[0]
    row = lax.broadcasted_iota(jnp.int32, (chunk, chunk), 0)
    col = lax.broadcasted_iota(jnp.int32, (chunk, chunk), 1)
    eye = row == col
    lane = lax.broadcasted_iota(jnp.int32, gf_ref.shape, 1)

    def gate_cols(g_ref, first, head):
        g = g_ref[...] + bias_ref[...]
        g = GATE_CAP * jnp.tanh(g / GATE_CAP)
        ig = jnp.sum(jnp.where(lane == first * ML_HEADS + head, g, 0.0), axis=1, keepdims=True)
        fg = jnp.sum(jnp.where(lane == (first + 1) * ML_HEADS + head, g, 0.0), axis=1, keepdims=True)
        return ig, jax.nn.log_sigmoid(fg)

    def run(i, q_ref, k_ref, v_ref, g_ref, first, mask, c_ref, n_ref, m_ref, h_ref):
        ig, lf = gate_cols(g_ref, first, pl.program_id(1) * heads_per_step + i)
        qk_cols, v_cols = slice(i * dk, (i + 1) * dk), slice(i * dv, (i + 1) * dv)
        q = q_ref[:, qk_cols] * dk_scale
        h_ref[:, v_cols] = _mlstm_chunk(q, k_ref[:, qk_cols], v_ref[:, v_cols], ig, lf, mask, eye,
                                        c_ref.at[i], n_ref.at[i], m_ref.at[i])

    for i in range(heads_per_step):
        run(i, qf_ref, kf_ref, vf_ref, gf_ref, 0, col <= row, cf_ref, nf_ref, mf_ref, hf_ref)
        run(i, qb_ref, kb_ref, vb_ref, gb_ref, 2, col >= row, cb_ref, nb_ref, mb_ref, hb_ref)


def mlstm_scan(qkv, gates, gate_bias, *, lay, dk, dv):
    r = qkv.shape[0]
    n_gates = gates.shape[1]
    st = lay.seq_tiles

    def fwd_tile(b, c):
        return jnp.where(c == 0, lay.ctx_tile(b), b * st + c - 1)

    def bwd_tile(b, c):
        return jnp.where(c == 0, lay.ctx_tile(b), b * st + st - c)

    hp = SCAN_HEADS_PER_STEP
    assert ML_HEADS % hp == 0

    def specs(tile):
        return [
            pl.BlockSpec((ROW_TILE, hp * dk), lambda b, h, c: (tile(b, c), h)),
            pl.BlockSpec((ROW_TILE, hp * dk), lambda b, h, c: (tile(b, c), ML_HEADS // hp + h)),
            pl.BlockSpec((ROW_TILE, hp * dv), lambda b, h, c: (tile(b, c), (2 * ML_HEADS * dk) // (hp * dv) + h)),
            pl.BlockSpec((ROW_TILE, n_gates), lambda b, h, c: (tile(b, c), 0)),
        ]

    def out_spec(tile):
        return pl.BlockSpec((ROW_TILE, hp * dv), lambda b, h, c: (tile(b, c), h))

    state = [pltpu.VMEM((hp, dk, dv), F32), pltpu.VMEM((hp, 1, dk), F32), pltpu.VMEM((hp, 1, 1), F32)]
    return pl.pallas_call(
        functools.partial(_mlstm_kernel, dk_scale=dk ** -0.5),
        grid=(lay.batch, ML_HEADS // hp, st + 1),
        in_specs=specs(fwd_tile) + specs(bwd_tile) + [pl.BlockSpec((1, n_gates), lambda b, h, c: (0, 0))],
        out_specs=[out_spec(fwd_tile), out_spec(bwd_tile)],
        out_shape=[jax.ShapeDtypeStruct((r, ML_HEADS * dv), F32)] * 2,
        scratch_shapes=state + state,
        compiler_params=_params("arbitrary", "arbitrary", "arbitrary"),
        name="mlstm_scan",
    )(qkv, qkv, qkv, gates, qkv, qkv, qkv, gates, gate_bias)


def _readout_kernel(hf_ref, hb_ref, o_ref, g_ref, out_ref, *, dv):
    for hd in range(ML_HEADS):
        sl = slice(hd * dv, (hd + 1) * dv)
        hs = hf_ref[:, sl] + hb_ref[:, sl]
        o = o_ref[:, sl]
        out_ref[:, sl] = (jax.nn.sigmoid(o) * _rms(hs, g_ref[:, sl])).astype(out_ref.dtype)


def mlstm_readout(h_f, h_b, o, g_norm, *, dv):
    r, width = h_f.shape
    row = pl.BlockSpec((ROW_TILE, width), lambda t: (t, 0))
    return pl.pallas_call(
        functools.partial(_readout_kernel, dv=dv),
        grid=(r // ROW_TILE,),
        in_specs=[row, row, row, pl.BlockSpec((1, width), lambda t: (0, 0))],
        out_specs=row,
        out_shape=jax.ShapeDtypeStruct((r, width), BF16),
        compiler_params=_params("arbitrary"),
        name="mlstm_readout",
    )(h_f, h_b, o, g_norm)


def _mla_prep_kernel(down_ref, qn_ref, kvn_ref, cos_ref, sin_ref, cq_ref, ckv_ref, kr_ref, *, q_lora, kv_lora):
    cq_ref[...] = _rms(down_ref[:, :q_lora], qn_ref[...]).astype(cq_ref.dtype)
    ckv_ref[...] = _rms(down_ref[:, q_lora:q_lora + kv_lora], kvn_ref[...]).astype(ckv_ref.dtype)
    kr = down_ref[:, q_lora + kv_lora:q_lora + kv_lora + LANE]
    lane = lax.broadcasted_iota(jnp.int32, kr.shape, 1)
    kr = jnp.where(lane < MLA_ROPE, kr, 0.0)
    kr_ref[...] = _rope_lanes(kr, cos_ref[...], sin_ref[...]).astype(kr_ref.dtype)


def mla_prep(down, q_norm, kv_norm, cos, sin, *, q_lora, kv_lora):
    r, width = down.shape
    return pl.pallas_call(
        functools.partial(_mla_prep_kernel, q_lora=q_lora, kv_lora=kv_lora),
        grid=(r // ROW_TILE,),
        in_specs=[pl.BlockSpec((ROW_TILE, width), lambda t: (t, 0)),
                  pl.BlockSpec((1, q_lora), lambda t: (0, 0)),
                  pl.BlockSpec((1, kv_lora), lambda t: (0, 0)),
                  pl.BlockSpec((ROW_TILE, LANE), lambda t: (t, 0)),
                  pl.BlockSpec((ROW_TILE, LANE), lambda t: (t, 0))],
        out_specs=[pl.BlockSpec((ROW_TILE, q_lora), lambda t: (t, 0)),
                   pl.BlockSpec((ROW_TILE, kv_lora), lambda t: (t, 0)),
                   pl.BlockSpec((ROW_TILE, LANE), lambda t: (t, 0))],
        out_shape=[jax.ShapeDtypeStruct((r, q_lora), BF16),
                   jax.ShapeDtypeStruct((r, kv_lora), BF16),
                   jax.ShapeDtypeStruct((r, LANE), BF16)],
        compiler_params=_params("arbitrary"),
        name="mla_prep",
    )(down, q_norm, kv_norm, cos, sin)


def _softmax_attend(q, k_t, v):
    s = jnp.dot(q, k_t, preferred_element_type=F32)
    p = jnp.exp2(s - jnp.max(s, axis=1, keepdims=True))
    o = jnp.dot(p.astype(BF16), v, preferred_element_type=F32)
    return o / jnp.sum(p, axis=1, keepdims=True)


def _attn_lat_kernel(q_ref, knl_ref, krl_ref, vl_ref, knc_ref, krc_ref, vc_ref, o_ref, kcat_ref, vcat_ref):
    ctx_len = vc_ref.shape[0]

    @pl.when(pl.program_id(2) == 0)
    def _():
        kcat_ref[:MLA_NOPE, :ctx_len] = knc_ref[...]
        kcat_ref[MLA_NOPE:, :ctx_len] = krc_ref[...]
        kcat_ref[:MLA_NOPE, ctx_len:] = knl_ref[...]
        kcat_ref[MLA_NOPE:, ctx_len:] = krl_ref[...]
        vcat_ref[:ctx_len, :] = vc_ref[...]
        vcat_ref[ctx_len:, :] = vl_ref[...]

    k_t, v = kcat_ref[...], vcat_ref[...]
    sub = min(ATTN_SUB_TILE, q_ref.shape[0])
    for part in range(q_ref.shape[0] // sub):
        rows = slice(part * sub, (part + 1) * sub)
        o_ref[rows, :] = _softmax_attend(q_ref[rows, :], k_t, v).astype(o_ref.dtype)


def _attn_ctx_kernel(q_ref, kn_ref, kr_ref, v_ref, prev_ref, o_ref):
    del prev_ref
    k_t = jnp.concatenate([kn_ref[...], kr_ref[...]], axis=0)
    o_ref[...] = _softmax_attend(q_ref[...], k_t, v_ref[...]).astype(o_ref.dtype)


def mla_attention(q, k_t, kr_t, v, *, lay, heads, with_ctx):
    seq = lay.seq_tiles * ROW_TILE
    tq = min(ATTN_Q_TILE, seq)
    q_tiles = seq // tq
    out_rows = lay.rows if with_ctx else lay.lat_rows

    def keys_lat(row):
        return pl.BlockSpec((MLA_NOPE, seq), lambda b, h, j: (row(h), b))

    def keys_ctx(row):
        return pl.BlockSpec((MLA_NOPE, ROW_TILE), lambda b, h, j: (row(h), lay.ctx_tile(b)))

    own, shared = (lambda h: h), (lambda h: 0)
    att = pl.pallas_call(
        _attn_lat_kernel,
        grid=(lay.batch, heads, q_tiles),
        in_specs=[pl.BlockSpec((tq, MLA_QW), lambda b, h, j: (b * q_tiles + j, h)),
                  keys_lat(own), keys_lat(shared), pl.BlockSpec((seq, MLA_V), lambda b, h, j: (b, h)),
                  keys_ctx(own), keys_ctx(shared),
                  pl.BlockSpec((ROW_TILE, MLA_V), lambda b, h, j: (lay.ctx_tile(b), h))],
        out_specs=pl.BlockSpec((tq, MLA_V), lambda b, h, j: (b * q_tiles + j, h)),
        out_shape=jax.ShapeDtypeStruct((out_rows, heads * MLA_V), BF16),
        scratch_shapes=[pltpu.VMEM((MLA_QW, ROW_TILE + seq), BF16), pltpu.VMEM((ROW_TILE + seq, MLA_V), BF16)],
        compiler_params=_params("arbitrary", "arbitrary", "arbitrary"),
        name="mla_attention",
    )(q, k_t, kr_t, v, k_t, kr_t, v)
    if not with_ctx:
        return att

    def ctx_keys(row):
        return pl.BlockSpec((MLA_NOPE, ROW_TILE), lambda b, h: (row(h), lay.ctx_tile(b)))

    def ctx_rows(width):
        return pl.BlockSpec((ROW_TILE, width), lambda b, h: (lay.ctx_tile(b), h))

    return pl.pallas_call(
        _attn_ctx_kernel,
        grid=(lay.batch, heads),
        in_specs=[ctx_rows(MLA_QW), ctx_keys(own), ctx_keys(shared), ctx_rows(MLA_V),
                  pl.BlockSpec(memory_space=pl.ANY)],
        out_specs=ctx_rows(MLA_V),
        out_shape=jax.ShapeDtypeStruct(att.shape, att.dtype),
        input_output_aliases={4: 0},
        compiler_params=_params("arbitrary", "arbitrary"),
        name="mla_attention_ctx",
    )(q, k_t, kr_t, v, att)


def _conv_gate_kernel(ug_ref, uv_ref, pg_ref, pv_ref, ng_ref, nv_ref, wg_ref, wv_ref, bg_ref, bv_ref, o_ref, *, lay):
    t = pl.program_id(0)
    first, last = lay.seq_first(t), lay.seq_last(t)
    rows = ug_ref.shape[0]
    ridx = lax.broadcasted_iota(jnp.int32, ug_ref.shape, 0)

    def conv(u_ref, p_ref, n_ref, w_ref, b_ref):
        u = u_ref[...]
        prev_row = jnp.where(first, 0.0, p_ref[SUBLANE - 1:SUBLANE, :])
        next_row = jnp.where(last, 0.0, n_ref[0:1, :])
        up = jnp.where(ridx == 0, prev_row, pltpu.roll(u, 1, 0))
        un = jnp.where(ridx == rows - 1, next_row, pltpu.roll(u, rows - 1, 0))
        return up * w_ref[0:1, :] + u * w_ref[1:2, :] + un * w_ref[2:3, :] + b_ref[...]

    gate = conv(ug_ref, pg_ref, ng_ref, wg_ref, bg_ref)
    val = conv(uv_ref, pv_ref, nv_ref, wv_ref, bv_ref)
    o_ref[...] = (gate * jax.nn.sigmoid(gate) * val).astype(o_ref.dtype)


def conv_gate(u, conv_w, conv_b, *, lay, tn=1024):
    r, two_f = u.shape
    f = two_f // 2
    nb = f // tn
    per_tile = ROW_TILE // SUBLANE
    last_sub = r // SUBLANE - 1

    def main(off):
        return pl.BlockSpec((ROW_TILE, tn), lambda t, j: (t, off + j))

    def prev(off):
        return pl.BlockSpec((SUBLANE, tn), lambda t, j: (jnp.maximum(t * per_tile - 1, 0), off + j))

    def nxt(off):
        return pl.BlockSpec((SUBLANE, tn), lambda t, j: (jnp.minimum((t + 1) * per_tile, last_sub), off + j))

    def wspec(off):
        return pl.BlockSpec((3, tn), lambda t, j: (0, off + j))

    def bspec(off):
        return pl.BlockSpec((1, tn), lambda t, j: (0, off + j))

    return pl.pallas_call(
        functools.partial(_conv_gate_kernel, lay=lay),
        grid=(r // ROW_TILE, nb),
        in_specs=[main(0), main(nb), prev(0), prev(nb), nxt(0), nxt(nb), wspec(0), wspec(nb), bspec(0), bspec(nb)],
        out_specs=pl.BlockSpec((ROW_TILE, tn), lambda t, j: (t, j)),
        out_shape=jax.ShapeDtypeStruct((r, f), BF16),
        compiler_params=_params("arbitrary", "arbitrary"),
        name="conv_gate",
    )(u, u, u, u, u, u, conv_w, conv_w, conv_b.reshape(1, two_f), conv_b.reshape(1, two_f))


def _rope_tables(batch, ctx_len, seq):
    per_axis = MLA_ROPE // 2
    inv_freq = ROPE_THETA ** (-jnp.arange(0, per_axis, 2, dtype=F32) / per_axis)
    pos = jnp.arange(seq)
    ang_r = (pos // GRID_W).astype(F32)[:, None] * inv_freq
    ang_c = (pos % GRID_W).astype(F32)[:, None] * inv_freq
    cos64 = jnp.concatenate([jnp.cos(ang_r)] * 2 + [jnp.cos(ang_c)] * 2, axis=1)
    sin64 = jnp.concatenate([-jnp.sin(ang_r), jnp.sin(ang_r), -jnp.sin(ang_c), jnp.sin(ang_c)], axis=1)
    cos_lat = jnp.tile(jnp.concatenate([cos64, cos64], axis=1), (batch, 1))
    sin_lat = jnp.tile(jnp.concatenate([sin64, sin64], axis=1), (batch, 1))
    cos = jnp.concatenate([cos_lat, jnp.ones((batch * ctx_len, LANE), F32)], axis=0)
    sin = jnp.concatenate([sin_lat, jnp.zeros((batch * ctx_len, LANE), F32)], axis=0)
    return cos, sin


def kernel(x, c, ctx, c_ctx, mod_w, mod_b, norm_w, ml_w_in, ml_b_gate, ml_norm, ml_w_out, mla_w_down, mla_q_norm,
           mla_w_uq, mla_kv_norm, mla_w_ukv, mla_w_out, ffn_w_up, ffn_conv_w, ffn_conv_b, ffn_w_down):
    batch, seq, d = x.shape
    ctx_len = ctx.shape[1]
    depth = mod_w.shape[0]
    lay = Layout(batch, seq, ctx_len)
    ml_qk = d // 2
    dk, dv = ml_qk // ML_HEADS, d // ML_HEADS
    heads = d // MLA_NOPE
    q_lora, kv_lora = d // 4, d // 8

    h = jnp.concatenate([x.reshape(batch * seq, d), ctx.reshape(batch * ctx_len, d)], axis=0)
    cond = jnp.concatenate([c_ctx[None], c, jnp.zeros((MOD_ROWS - 1 - batch, d), F32)], axis=0)
    mods = adaln_all(cond, mod_w, mod_b).reshape(depth * MOD_ROWS * N_MOD, 1, d)
    gains = norm_w.reshape(depth * 4, 1, d)
    cos, sin = _rope_tables(batch, ctx_len, seq)
    norm = functools.partial(norm_modulate, mods=mods, norm_w=gains, lay=lay)
    w_in_t = jnp.swapaxes(ml_w_in, 1, 2)

    f = None
    for i in range(depth):
        j = i // 2
        ctx_out = i < depth - 1
        if f is None:
            a = norm(h, layer=i, rows=lay.rows, pre_idx=0, shift_idx=0, scale_idx=1)[0]
        else:
            h, a = norm(h, layer=i, rows=lay.rows, y=f, y_layer=i - 1, gate_idx=5, post_idx=3,
                        pre_idx=0, shift_idx=0, scale_idx=1)
        if i % 2 == 0:
            qkv = matmul(a, w_in_t, layer=j, w_t=True, n_out=2 * ml_qk + d, out_dtype=BF16)
            o = matmul(a, w_in_t, layer=j, w_t=True, n_out=d, col0=2 * ml_qk + d)
            gates = matmul(a, w_in_t, layer=j, w_t=True, n_out=4 * ML_HEADS, col0=2 * ml_qk + 2 * d)
            bias = ml_b_gate[j].reshape(1, 4 * ML_HEADS)
            h_f, h_b = mlstm_scan(qkv, gates, bias, lay=lay, dk=dk, dv=dv)
            mixed = mlstm_readout(h_f, h_b, o, ml_norm[j].reshape(1, d), dv=dv)
        else:
            down_w = mla_w_down[j]
            down = matmul(a, jnp.pad(down_w, ((0, 0), (0, -down_w.shape[1] % 512))))
            cq, ckv, kr = mla_prep(down, mla_q_norm[j].reshape(1, q_lora), mla_kv_norm[j].reshape(1, kv_lora),
                                   cos, sin, q_lora=q_lora, kv_lora=kv_lora)
            w_q = jnp.pad(mla_w_uq[j].reshape(q_lora, heads, MLA_NOPE + MLA_ROPE),
                          ((0, 0), (0, 0), (0, MLA_QW - MLA_NOPE - MLA_ROPE))).reshape(q_lora, heads * MLA_QW)
            q = matmul(cq, w_q, out_dtype=BF16, rope=(cos, sin), rope_scale=MLA_SCALE * math.log2(math.e))
            w_kv = mla_w_ukv[j].reshape(kv_lora, heads, MLA_NOPE + MLA_V)
            w_k_t = jnp.transpose(w_kv[:, :, :MLA_NOPE].reshape(kv_lora, heads * MLA_NOPE))
            w_v = w_kv[:, :, MLA_NOPE:].reshape(kv_lora, heads * MLA_V)
            k_t = matmul(w_k_t, ckv, w_t=True, out_dtype=BF16)
            v = matmul(ckv, w_v, out_dtype=BF16)
            mixed = mla_attention(q, k_t, jnp.transpose(kr), v, lay=lay, heads=heads, with_ctx=ctx_out)
        rows = mixed.shape[0]
        y = matmul(mixed, ml_w_out if i % 2 == 0 else mla_w_out, layer=j)
        h, a = norm(h, layer=i, rows=rows, y=y, gate_idx=2, post_idx=1, pre_idx=2, shift_idx=3, scale_idx=4)
        u = matmul(a, ffn_w_up, layer=i)
        g = conv_gate(u, ffn_conv_w[i], ffn_conv_b[i], lay=lay)
        f = matmul(g, ffn_w_down, layer=i)
    out = norm(h, layer=depth - 1, rows=lay.lat_rows, y=f, gate_idx=5, post_idx=3)[0]
    return out.reshape(batch, seq, d)
```

```python
import functools
import math

import jax
import jax.numpy as jnp
import numpy as np
from jax import lax
from jax.experimental import pallas as pl
from jax.experimental.pallas import tpu as pltpu

F32 = jnp.float32
BF16 = jnp.bfloat16

LANE = 128
SUBLANE = 8
VMEM_LIMIT_BYTES = 56 * 1024 * 1024
W_BLOCK_BYTES = 8 * 1024 * 1024

EPS = 1e-6
ML_HEADS = 8
GATE_CAP = 15.0
GRID_W = 64
ROPE_THETA = 10000.0
MLA_NOPE = 128
MLA_ROPE = 64
MLA_V = 128
MLA_SCALE = (MLA_NOPE + MLA_ROPE) ** -0.5
MLA_QW = 2 * LANE
N_MOD = 6
MOD_ROWS = 8

ROW_TILE = 256
ATTN_Q_TILE = 2048
ATTN_SUB_TILE = 256
SCAN_HEADS_PER_STEP = 2


def _params(*sem):
    return pltpu.CompilerParams(dimension_semantics=sem, vmem_limit_bytes=VMEM_LIMIT_BYTES)


class Layout:
    def __init__(self, batch, seq, ctx_len):
        assert ctx_len == ROW_TILE and seq % (2 * ROW_TILE) == 0 and batch + 1 <= MOD_ROWS
        self.batch = batch
        self.seq_tiles = seq // ROW_TILE
        self.lat_tiles = batch * self.seq_tiles
        self.tiles = self.lat_tiles + batch
        self.lat_rows = batch * seq
        self.rows = self.tiles * ROW_TILE

    def mod_row(self, t):
        return jnp.where(t >= self.lat_tiles, 0, 1 + t // self.seq_tiles)

    def seq_first(self, t):
        return jnp.logical_or(t >= self.lat_tiles, t % self.seq_tiles == 0)

    def seq_last(self, t):
        return jnp.logical_or(t >= self.lat_tiles, t % self.seq_tiles == self.seq_tiles - 1)

    def ctx_tile(self, b):
        return self.lat_tiles + b


def _adaln_kernel(c_ref, w_ref, b_ref, o_ref):
    c = c_ref[...]
    s = c * jax.nn.sigmoid(c)
    o_ref[0] = jnp.dot(s, w_ref[0], preferred_element_type=F32) + b_ref[0]


def adaln_all(cond, mod_w, mod_b, tn=1024):
    depth, d, n = mod_w.shape
    return pl.pallas_call(
        _adaln_kernel,
        grid=(depth, n // tn),
        in_specs=[
            pl.BlockSpec((MOD_ROWS, d), lambda l, j: (0, 0)),
            pl.BlockSpec((1, d, tn), lambda l, j: (l, 0, j)),
            pl.BlockSpec((1, 1, tn), lambda l, j: (l, 0, j)),
        ],
        out_specs=pl.BlockSpec((1, MOD_ROWS, tn), lambda l, j: (l, 0, j)),
        out_shape=jax.ShapeDtypeStruct((depth, MOD_ROWS, n), F32),
        compiler_params=_params("arbitrary", "arbitrary"),
        name="adaln",
    )(cond, mod_w, mod_b.reshape(depth, 1, n))


def _rms(x, g):
    return x * lax.rsqrt(jnp.mean(x * x, axis=-1, keepdims=True) + EPS) * g


def _norm_kernel(*refs, has_y, has_a, split_tiles):
    it = iter(refs)
    if split_tiles is None:
        h = next(it)[...]
    else:
        lat_ref, ctx_ref = next(it), next(it)
        h = jnp.where(pl.program_id(0) < split_tiles, lat_ref[...], ctx_ref[...])
    if has_y:
        y_ref, gate_ref, gpost_ref = next(it), next(it), next(it)
    if has_a:
        gpre_ref, shift_ref, scale_ref = next(it), next(it), next(it)
    if has_y:
        hout_ref = next(it)
    if has_a:
        a_ref = next(it)
    if has_y:
        h = h + gate_ref[0] * _rms(y_ref[...], gpost_ref[0])
        hout_ref[...] = h
    if has_a:
        a = _rms(h, gpre_ref[0]) * (1.0 + scale_ref[0]) + shift_ref[0]
        a_ref[...] = a.astype(a_ref.dtype)


def norm_modulate(h, mods, norm_w, layer, *, lay, rows, y=None, y_layer=None, gate_idx=None, post_idx=None,
                  pre_idx=None, shift_idx=None, scale_idx=None):
    split = isinstance(h, tuple)
    d = h[0].shape[1] if split else h.shape[1]
    has_y, has_a = y is not None, pre_idx is not None
    y_layer = layer if y_layer is None else y_layer
    row_spec = pl.BlockSpec((ROW_TILE, d), lambda t: (t, 0))

    def mod_spec(lyr, which):
        return pl.BlockSpec((1, 1, d), lambda t: ((lyr * MOD_ROWS + lay.mod_row(t)) * N_MOD + which, 0, 0))

    def gain_spec(lyr, which):
        return pl.BlockSpec((1, 1, d), lambda t: (lyr * 4 + which, 0, 0))

    if split:
        args = list(h)
        in_specs = [pl.BlockSpec((ROW_TILE, d), lambda t: (jnp.minimum(t, lay.lat_tiles - 1), 0)),
                    pl.BlockSpec((ROW_TILE, d), lambda t: (jnp.maximum(t - lay.lat_tiles, 0), 0))]
    else:
        args, in_specs = [h], [row_spec]
    if has_y:
        args += [y, mods, norm_w]
        in_specs += [row_spec, mod_spec(y_layer, gate_idx), gain_spec(y_layer, post_idx)]
    if has_a:
        args += [norm_w, mods, mods]
        in_specs += [gain_spec(layer, pre_idx), mod_spec(layer, shift_idx), mod_spec(layer, scale_idx)]
    out_shape, out_specs = [], []
    if has_y:
        out_shape.append(jax.ShapeDtypeStruct((rows, d), F32))
        out_specs.append(row_spec)
    if has_a:
        out_shape.append(jax.ShapeDtypeStruct((rows, d), BF16))
        out_specs.append(row_spec)
    return pl.pallas_call(
        functools.partial(_norm_kernel, has_y=has_y, has_a=has_a, split_tiles=lay.lat_tiles if split else None),
        grid=(rows // ROW_TILE,),
        in_specs=in_specs,
        out_specs=out_specs,
        out_shape=out_shape,
        compiler_params=_params("arbitrary"),
        name="norm_modulate",
    )(*args)


def _rope_lanes(x, cos, sin):
    lane = lax.broadcasted_iota(jnp.int32, x.shape, 1)
    partner = jnp.where(lane % 32 < 16, pltpu.roll(x, LANE - 16, 1), pltpu.roll(x, 16, 1))
    return x * cos + partner * sin


def _mm_kernel(*refs, rope_scale, w_t):
    if rope_scale is not None:
        a_ref, w_ref, cos_ref, sin_ref, o_ref = refs
    else:
        a_ref, w_ref, o_ref = refs
    a, w = a_ref[...].astype(BF16), w_ref[...].astype(BF16)
    if w_t:
        acc = lax.dot_general(a, w, (((1,), (1,)), ((), ())), preferred_element_type=F32)
    else:
        acc = jnp.dot(a, w, preferred_element_type=F32)
    if rope_scale is not None:
        cos, sin = cos_ref[...], sin_ref[...]
        for g in range(acc.shape[1] // LANE):
            blk = acc[:, g * LANE:(g + 1) * LANE]
            if g % 2 == 1:
                blk = _rope_lanes(blk, cos, sin)
            o_ref[:, g * LANE:(g + 1) * LANE] = (blk * rope_scale).astype(o_ref.dtype)
    else:
        o_ref[...] = acc.astype(o_ref.dtype)


def _pick_tm(r, cap):
    return max(tm for tm in range(16, cap + 1, 16) if r % tm == 0)


def _pick_tn(n, k):
    tn = 256
    while n % (2 * tn) == 0 and 2 * tn * k * 4 <= W_BLOCK_BYTES and 2 * tn <= 2048:
        tn *= 2
    return min(tn, n)


def matmul(a, w, *, layer=None, n_out=None, w_t=False, col0=0, tm_cap=1088, out_dtype=F32, rope=None,
           rope_scale=None):
    r, k = a.shape
    n_w, k_w = (w.shape[-2], w.shape[-1]) if w_t else (w.shape[-1], w.shape[-2])
    n_out = n_w if n_out is None else n_out
    tn = _pick_tn(n_out, k)
    assert n_out % tn == 0 and col0 % tn == 0 and k_w == k
    cb0 = col0 // tn
    tm = _pick_tm(r, tm_cap)
    w_block = (tn, k) if w_t else (k, tn)
    w_index = (lambda j: (cb0 + j, 0)) if w_t else (lambda j: (0, cb0 + j))
    if layer is None:
        w_spec = pl.BlockSpec(w_block, lambda i, j: w_index(j))
    else:
        w_spec = pl.BlockSpec((None,) + w_block, lambda i, j: (layer,) + w_index(j))
    args = [a, w]
    in_specs = [pl.BlockSpec((tm, k), lambda i, j: (i, 0)), w_spec]
    if rope is not None:
        args += list(rope)
        in_specs += [pl.BlockSpec((tm, LANE), lambda i, j: (i, 0))] * 2
    return pl.pallas_call(
        functools.partial(_mm_kernel, rope_scale=rope_scale if rope is not None else None, w_t=w_t),
        grid=(r // tm, n_out // tn),
        in_specs=in_specs,
        out_specs=pl.BlockSpec((tm, tn), lambda i, j: (i, j)),
        out_shape=jax.ShapeDtypeStruct((r, n_out), out_dtype),
        compiler_params=_params("arbitrary", "arbitrary"),
        name="matmul",
    )(*args)


def _col_to_row(col, eye):
    return jnp.sum(jnp.where(eye, col, 0.0), axis=0, keepdims=True)


def _mlstm_chunk(q, k, v, ig, lf, mask, eye, c_ref, n_ref, m_ref):
    neg_inf = -jnp.inf
    lf_row = _col_to_row(lf, eye)
    b = jnp.sum(jnp.where(mask, lf_row, 0.0), axis=1, keepdims=True)
    b_end = jnp.sum(lf, axis=0, keepdims=True)
    src = ig - b
    dmat = jnp.where(mask, b + _col_to_row(src, eye), neg_inf)
    m_prev = m_ref[...]
    inter = b + m_prev
    m_out = jnp.maximum(inter, jnp.max(dmat, axis=1, keepdims=True))
    w_intra = jnp.exp(dmat - m_out)
    w_inter = jnp.exp(inter - m_out)
    qk = lax.dot_general(q, k, (((1,), (1,)), ((), ())), preferred_element_type=F32)
    s = qk * w_intra
    c_prev = c_ref[...]
    num = jnp.dot(s.astype(BF16), v, preferred_element_type=F32)
    num = num + w_inter * jnp.dot(q, c_prev.astype(BF16), preferred_element_type=F32)
    qn = jnp.sum(q.astype(F32) * n_ref[...], axis=1, keepdims=True)
    den = jnp.sum(s, axis=1, keepdims=True) + w_inter * qn
    den = jnp.maximum(jnp.abs(den), jnp.exp(-m_out))
    h = num / den
    g = b_end + src
    m_new = jnp.maximum(b_end + m_prev, jnp.max(g, axis=0, keepdims=True))
    wg = jnp.exp(g - m_new)
    decay = jnp.exp(b_end + m_prev - m_new)
    kw = k.astype(F32) * wg
    c_ref[...] = decay * c_prev + lax.dot_general(
        kw.astype(BF16), v, (((0,), (0,)), ((), ())), preferred_element_type=F32)
    n_ref[...] = decay * n_ref[...] + jnp.sum(kw, axis=0, keepdims=True)
    m_ref[...] = m_new
    return h


def _mlstm_kernel(qf_ref, kf_ref, vf_ref, gf_ref, qb_ref, kb_ref, vb_ref, gb_ref, bias_ref,
                  hf_ref, hb_ref, cf_ref, nf_ref, mf_ref, cb_ref, nb_ref, mb_ref, *, dk_scale):
    heads_per_step = cf_ref.shape[0]
    dk, dv = cf_ref.shape[1], cf_ref.shape[2]

    @pl.when(pl.program_id(2) == 0)
    def _():
        for ref in (cf_ref, nf_ref, mf_ref, cb_ref, nb_ref, mb_ref):
            ref[...] = jnp.zeros_like(ref)

    chunk = qf_ref.shape[0]
    row = lax.broadcasted_iota(jnp.int32, (chunk, chunk), 0)
    col = lax.broadcasted_iota(jnp.int32, (chunk, chunk), 1)
    eye = row == col
    lane = lax.broadcasted_iota(jnp.int32, gf_ref.shape, 1)

    def gate_cols(g_ref, first, head):
        g = g_ref[...] + bias_ref[...]
        g = GATE_CAP * jnp.tanh(g / GATE_CAP)
        ig = jnp.sum(jnp.where(lane == first * ML_HEADS + head, g, 0.0), axis=1, keepdims=True)
        fg = jnp.sum(jnp.where(lane == (first + 1) * ML_HEADS + head, g, 0.0), axis=1, keepdims=True)
        return ig, jax.nn.log_sigmoid(fg)

    def run(i, q_ref, k_ref, v_ref, g_ref, first, mask, c_ref, n_ref, m_ref, h_ref):
        ig, lf = gate_cols(g_ref, first, pl.program_id(1) * heads_per_step + i)
        qk_cols, v_cols = slice(i * dk, (i + 1) * dk), slice(i * dv, (i + 1) * dv)
        q = q_ref[:, qk_cols] * dk_scale
        h_ref[:, v_cols] = _mlstm_chunk(q, k_ref[:, qk_cols], v_ref[:, v_cols], ig, lf, mask, eye,
                                        c_ref.at[i], n_ref.at[i], m_ref.at[i])

    for i in range(heads_per_step):
        run(i, qf_ref, kf_ref, vf_ref, gf_ref, 0, col <= row, cf_ref, nf_ref, mf_ref, hf_ref)
        run(i, qb_ref, kb_ref, vb_ref, gb_ref, 2, col >= row, cb_ref, nb_ref, mb_ref, hb_ref)


def mlstm_scan(qkv, gates, gate_bias, *, lay, dk, dv):
    r = qkv.shape[0]
    n_gates = gates.shape[1]
    st = lay.seq_tiles

    def fwd_tile(b, c):
        return jnp.where(c == 0, lay.ctx_tile(b), b * st + c - 1)

    def bwd_tile(b, c):
        return jnp.where(c == 0, lay.ctx_tile(b), b * st + st - c)

    hp = SCAN_HEADS_PER_STEP
    assert ML_HEADS % hp == 0

    def specs(tile):
        return [
            pl.BlockSpec((ROW_TILE, hp * dk), lambda b, h, c: (tile(b, c), h)),
            pl.BlockSpec((ROW_TILE, hp * dk), lambda b, h, c: (tile(b, c), ML_HEADS // hp + h)),
            pl.BlockSpec((ROW_TILE, hp * dv), lambda b, h, c: (tile(b, c), (2 * ML_HEADS * dk) // (hp * dv) + h)),
            pl.BlockSpec((ROW_TILE, n_gates), lambda b, h, c: (tile(b, c), 0)),
        ]

    def out_spec(tile):
        return pl.BlockSpec((ROW_TILE, hp * dv), lambda b, h, c: (tile(b, c), h))

    state = [pltpu.VMEM((hp, dk, dv), F32), pltpu.VMEM((hp, 1, dk), F32), pltpu.VMEM((hp, 1, 1), F32)]
    return pl.pallas_call(
        functools.partial(_mlstm_kernel, dk_scale=dk ** -0.5),
        grid=(lay.batch, ML_HEADS // hp, st + 1),
        in_specs=specs(fwd_tile) + specs(bwd_tile) + [pl.BlockSpec((1, n_gates), lambda b, h, c: (0, 0))],
        out_specs=[out_spec(fwd_tile), out_spec(bwd_tile)],
        out_shape=[jax.ShapeDtypeStruct((r, ML_HEADS * dv), F32)] * 2,
        scratch_shapes=state + state,
        compiler_params=_params("arbitrary", "arbitrary", "arbitrary"),
        name="mlstm_scan",
    )(qkv, qkv, qkv, gates, qkv, qkv, qkv, gates, gate_bias)


def _readout_kernel(hf_ref, hb_ref, o_ref, g_ref, out_ref, *, dv):
    for hd in range(ML_HEADS):
        sl = slice(hd * dv, (hd + 1) * dv)
        hs = hf_ref[:, sl] + hb_ref[:, sl]
        o = o_ref[:, sl]
        out_ref[:, sl] = (jax.nn.sigmoid(o) * _rms(hs, g_ref[:, sl])).astype(out_ref.dtype)


def mlstm_readout(h_f, h_b, o, g_norm, *, dv):
    r, width = h_f.shape
    row = pl.BlockSpec((ROW_TILE, width), lambda t: (t, 0))
    return pl.pallas_call(
        functools.partial(_readout_kernel, dv=dv),
        grid=(r // ROW_TILE,),
        in_specs=[row, row, row, pl.BlockSpec((1, width), lambda t: (0, 0))],
        out_specs=row,
        out_shape=jax.ShapeDtypeStruct((r, width), BF16),
        compiler_params=_params("arbitrary"),
        name="mlstm_readout",
    )(h_f, h_b, o, g_norm)


def _mla_prep_kernel(down_ref, qn_ref, kvn_ref, cos_ref, sin_ref, cq_ref, ckv_ref, kr_ref, *, q_lora, kv_lora):
    cq_ref[...] = _rms(down_ref[:, :q_lora], qn_ref[...]).astype(cq_ref.dtype)
    ckv_ref[...] = _rms(down_ref[:, q_lora:q_lora + kv_lora], kvn_ref[...]).astype(ckv_ref.dtype)
    kr = down_ref[:, q_lora + kv_lora:q_lora + kv_lora + LANE]
    lane = lax.broadcasted_iota(jnp.int32, kr.shape, 1)
    kr = jnp.where(lane < MLA_ROPE, kr, 0.0)
    kr_ref[...] = _rope_lanes(kr, cos_ref[...], sin_ref[...]).astype(kr_ref.dtype)


def mla_prep(down, q_norm, kv_norm, cos, sin, *, q_lora, kv_lora):
    r, width = down.shape
    return pl.pallas_call(
        functools.partial(_mla_prep_kernel, q_lora=q_lora, kv_lora=kv_lora),
        grid=(r // ROW_TILE,),
        in_specs=[pl.BlockSpec((ROW_TILE, width), lambda t: (t, 0)),
                  pl.BlockSpec((1, q_lora), lambda t: (0, 0)),
                  pl.BlockSpec((1, kv_lora), lambda t: (0, 0)),
                  pl.BlockSpec((ROW_TILE, LANE), lambda t: (t, 0)),
                  pl.BlockSpec((ROW_TILE, LANE), lambda t: (t, 0))],
        out_specs=[pl.BlockSpec((ROW_TILE, q_lora), lambda t: (t, 0)),
                   pl.BlockSpec((ROW_TILE, kv_lora), lambda t: (t, 0)),
                   pl.BlockSpec((ROW_TILE, LANE), lambda t: (t, 0))],
        out_shape=[jax.ShapeDtypeStruct((r, q_lora), BF16),
                   jax.ShapeDtypeStruct((r, kv_lora), BF16),
                   jax.ShapeDtypeStruct((r, LANE), BF16)],
        compiler_params=_params("arbitrary"),
        name="mla_prep",
    )(down, q_norm, kv_norm, cos, sin)


def _softmax_attend(q, k_t, v):
    s = jnp.dot(q, k_t, preferred_element_type=F32)
    p = jnp.exp2(s - jnp.max(s, axis=1, keepdims=True))
    o = jnp.dot(p.astype(BF16), v, preferred_element_type=F32)
    return o / jnp.sum(p, axis=1, keepdims=True)


def _attn_lat_kernel(q_ref, knl_ref, krl_ref, vl_ref, knc_ref, krc_ref, vc_ref, o_ref, kcat_ref, vcat_ref):
    ctx_len = vc_ref.shape[0]

    @pl.when(pl.program_id(2) == 0)
    def _():
        kcat_ref[:MLA_NOPE, :ctx_len] = knc_ref[...]
        kcat_ref[MLA_NOPE:, :ctx_len] = krc_ref[...]
        kcat_ref[:MLA_NOPE, ctx_len:] = knl_ref[...]
        kcat_ref[MLA_NOPE:, ctx_len:] = krl_ref[...]
        vcat_ref[:ctx_len, :] = vc_ref[...]
        vcat_ref[ctx_len:, :] = vl_ref[...]

    k_t, v = kcat_ref[...], vcat_ref[...]
    sub = min(ATTN_SUB_TILE, q_ref.shape[0])
    for part in range(q_ref.shape[0] // sub):
        rows = slice(part * sub, (part + 1) * sub)
        o_ref[rows, :] = _softmax_attend(q_ref[rows, :], k_t, v).astype(o_ref.dtype)


def _attn_ctx_kernel(q_ref, kn_ref, kr_ref, v_ref, prev_ref, o_ref):
    del prev_ref
    k_t = jnp.concatenate([kn_ref[...], kr_ref[...]], axis=0)
    o_ref[...] = _softmax_attend(q_ref[...], k_t, v_ref[...]).astype(o_ref.dtype)


def mla_attention(q, k_t, kr_t, v, *, lay, heads, with_ctx):
    seq = lay.seq_tiles * ROW_TILE
    tq = min(ATTN_Q_TILE, seq)
    q_tiles = seq // tq
    out_rows = lay.rows if with_ctx else lay.lat_rows

    def keys_lat(row):
        return pl.BlockSpec((MLA_NOPE, seq), lambda b, h, j: (row(h), b))

    def keys_ctx(row):
        return pl.BlockSpec((MLA_NOPE, ROW_TILE), lambda b, h, j: (row(h), lay.ctx_tile(b)))

    own, shared = (lambda h: h), (lambda h: 0)
    att = pl.pallas_call(
        _attn_lat_kernel,
        grid=(lay.batch, heads, q_tiles),
        in_specs=[pl.BlockSpec((tq, MLA_QW), lambda b, h, j: (b * q_tiles + j, h)),
                  keys_lat(own), keys_lat(shared), pl.BlockSpec((seq, MLA_V), lambda b, h, j: (b, h)),
                  keys_ctx(own), keys_ctx(shared),
                  pl.BlockSpec((ROW_TILE, MLA_V), lambda b, h, j: (lay.ctx_tile(b), h))],
        out_specs=pl.BlockSpec((tq, MLA_V), lambda b, h, j: (b * q_tiles + j, h)),
        out_shape=jax.ShapeDtypeStruct((out_rows, heads * MLA_V), BF16),
        scratch_shapes=[pltpu.VMEM((MLA_QW, ROW_TILE + seq), BF16), pltpu.VMEM((ROW_TILE + seq, MLA_V), BF16)],
        compiler_params=_params("arbitrary", "arbitrary", "arbitrary"),
        name="mla_attention",
    )(q, k_t, kr_t, v, k_t, kr_t, v)
    if not with_ctx:
        return att

    def ctx_keys(row):
        return pl.BlockSpec((MLA_NOPE, ROW_TILE), lambda b, h: (row(h), lay.ctx_tile(b)))

    def ctx_rows(width):
        return pl.BlockSpec((ROW_TILE, width), lambda b, h: (lay.ctx_tile(b), h))

    return pl.pallas_call(
        _attn_ctx_kernel,
        grid=(lay.batch, heads),
        in_specs=[ctx_rows(MLA_QW), ctx_keys(own), ctx_keys(shared), ctx_rows(MLA_V),
                  pl.BlockSpec(memory_space=pl.ANY)],
        out_specs=ctx_rows(MLA_V),
        out_shape=jax.ShapeDtypeStruct(att.shape, att.dtype),
        input_output_aliases={4: 0},
        compiler_params=_params("arbitrary", "arbitrary"),
        name="mla_attention_ctx",
    )(q, k_t, kr_t, v, att)


def _conv_gate_kernel(ug_ref, uv_ref, pg_ref, pv_ref, ng_ref, nv_ref, wg_ref, wv_ref, bg_ref, bv_ref, o_ref, *, lay):
    t = pl.program_id(0)
    first, last = lay.seq_first(t), lay.seq_last(t)
    rows = ug_ref.shape[0]
    ridx = lax.broadcasted_iota(jnp.int32, ug_ref.shape, 0)

    def conv(u_ref, p_ref, n_ref, w_ref, b_ref):
        u = u_ref[...]
        prev_row = jnp.where(first, 0.0, p_ref[SUBLANE - 1:SUBLANE, :])
        next_row = jnp.where(last, 0.0, n_ref[0:1, :])
        up = jnp.where(ridx == 0, prev_row, pltpu.roll(u, 1, 0))
        un = jnp.where(ridx == rows - 1, next_row, pltpu.roll(u, rows - 1, 0))
        return up * w_ref[0:1, :] + u * w_ref[1:2, :] + un * w_ref[2:3, :] + b_ref[...]

    gate = conv(ug_ref, pg_ref, ng_ref, wg_ref, bg_ref)
    val = conv(uv_ref, pv_ref, nv_ref, wv_ref, bv_ref)
    o_ref[...] = (gate * jax.nn.sigmoid(gate) * val).astype(o_ref.dtype)


def conv_gate(u, conv_w, conv_b, *, lay, tn=1024):
    r, two_f = u.shape
    f = two_f // 2
    nb = f // tn
    per_tile = ROW_TILE // SUBLANE
    last_sub = r // SUBLANE - 1

    def main(off):
        return pl.BlockSpec((ROW_TILE, tn), lambda t, j: (t, off + j))

    def prev(off):
        return pl.BlockSpec((SUBLANE, tn), lambda t, j: (jnp.maximum(t * per_tile - 1, 0), off + j))

    def nxt(off):
        return pl.BlockSpec((SUBLANE, tn), lambda t, j: (jnp.minimum((t + 1) * per_tile, last_sub), off + j))

    def wspec(off):
        return pl.BlockSpec((3, tn), lambda t, j: (0, off + j))

    def bspec(off):
        return pl.BlockSpec((1, tn), lambda t, j: (0, off + j))

    return pl.pallas_call(
        functools.partial(_conv_gate_kernel, lay=lay),
        grid=(r // ROW_TILE, nb),
        in_specs=[main(0), main(nb), prev(0), prev(nb), nxt(0), nxt(nb), wspec(0), wspec(nb), bspec(0), bspec(nb)],
        out_specs=pl.BlockSpec((ROW_TILE, tn), lambda t, j: (t, j)),
        out_shape=jax.ShapeDtypeStruct((r, f), BF16),
        compiler_params=_params("arbitrary", "arbitrary"),
        name="conv_gate",
    )(u, u, u, u, u, u, conv_w, conv_w, conv_b.reshape(1, two_f), conv_b.reshape(1, two_f))


def _rope_tables(batch, ctx_len, seq):
    per_axis = MLA_ROPE // 2
    inv_freq = np.float32(ROPE_THETA) ** (-np.arange(0, per_axis, 2, dtype=np.float32) / np.float32(per_axis))
    pos = np.arange(seq)
    ang_r = (pos // GRID_W).astype(np.float32)[:, None] * inv_freq
    ang_c = (pos % GRID_W).astype(np.float32)[:, None] * inv_freq
    cos64 = np.concatenate([np.cos(ang_r)] * 2 + [np.cos(ang_c)] * 2, axis=1)
    sin64 = np.concatenate([-np.sin(ang_r), np.sin(ang_r), -np.sin(ang_c), np.sin(ang_c)], axis=1)
    cos_lat = np.tile(np.concatenate([cos64, cos64], axis=1), (batch, 1))
    sin_lat = np.tile(np.concatenate([sin64, sin64], axis=1), (batch, 1))
    cos = np.concatenate([cos_lat, np.ones((batch * ctx_len, LANE), np.float32)], axis=0)
    sin = np.concatenate([sin_lat, np.zeros((batch * ctx_len, LANE), np.float32)], axis=0)
    return jnp.asarray(cos, F32), jnp.asarray(sin, F32)


def kernel(x, c, ctx, c_ctx, mod_w, mod_b, norm_w, ml_w_in, ml_b_gate, ml_norm, ml_w_out, mla_w_down, mla_q_norm,
           mla_w_uq, mla_kv_norm, mla_w_ukv, mla_w_out, ffn_w_up, ffn_conv_w, ffn_conv_b, ffn_w_down):
    batch, seq, d = x.shape
    ctx_len = ctx.shape[1]
    depth = mod_w.shape[0]
    lay = Layout(batch, seq, ctx_len)
    ml_qk = d // 2
    dk, dv = ml_qk // ML_HEADS, d // ML_HEADS
    heads = d // MLA_NOPE
    q_lora, kv_lora = d // 4, d // 8

    h = (x.reshape(batch * seq, d), ctx.reshape(batch * ctx_len, d))
    cond = jnp.concatenate([c_ctx[None], c, jnp.zeros((MOD_ROWS - 1 - batch, d), F32)], axis=0)
    mods = adaln_all(cond, mod_w, mod_b).reshape(depth * MOD_ROWS * N_MOD, 1, d)
    gains = norm_w.reshape(depth * 4, 1, d)
    cos, sin = _rope_tables(batch, ctx_len, seq)
    norm = functools.partial(norm_modulate, mods=mods, norm_w=gains, lay=lay)
    w_in_t = jnp.swapaxes(ml_w_in, 1, 2)

    f = None
    for i in range(depth):
        j = i // 2
        ctx_out = i < depth - 1
        if f is None:
            a = norm(h, layer=i, rows=lay.rows, pre_idx=0, shift_idx=0, scale_idx=1)[0]
        else:
            h, a = norm(h, layer=i, rows=lay.rows, y=f, y_layer=i - 1, gate_idx=5, post_idx=3,
                        pre_idx=0, shift_idx=0, scale_idx=1)
        if i % 2 == 0:
            qkv = matmul(a, w_in_t, layer=j, w_t=True, n_out=2 * ml_qk + d, out_dtype=BF16)
            o = matmul(a, w_in_t, layer=j, w_t=True, n_out=d, col0=2 * ml_qk + d)
            gates = matmul(a, w_in_t, layer=j, w_t=True, n_out=4 * ML_HEADS, col0=2 * ml_qk + 2 * d)
            bias = ml_b_gate[j].reshape(1, 4 * ML_HEADS)
            h_f, h_b = mlstm_scan(qkv, gates, bias, lay=lay, dk=dk, dv=dv)
            mixed = mlstm_readout(h_f, h_b, o, ml_norm[j].reshape(1, d), dv=dv)
        else:
            down_w = mla_w_down[j]
            down = matmul(a, jnp.pad(down_w, ((0, 0), (0, -down_w.shape[1] % 512))))
            cq, ckv, kr = mla_prep(down, mla_q_norm[j].reshape(1, q_lora), mla_kv_norm[j].reshape(1, kv_lora),
                                   cos, sin, q_lora=q_lora, kv_lora=kv_lora)
            w_q = jnp.pad(mla_w_uq[j].reshape(q_lora, heads, MLA_NOPE + MLA_ROPE),
                          ((0, 0), (0, 0), (0, MLA_QW - MLA_NOPE - MLA_ROPE))).reshape(q_lora, heads * MLA_QW)
            q = matmul(cq, w_q, out_dtype=BF16, rope=(cos, sin), rope_scale=MLA_SCALE * math.log2(math.e))
            w_kv = mla_w_ukv[j].reshape(kv_lora, heads, MLA_NOPE + MLA_V)
            w_k_t = jnp.transpose(w_kv[:, :, :MLA_NOPE].reshape(kv_lora, heads * MLA_NOPE))
            w_v = w_kv[:, :, MLA_NOPE:].reshape(kv_lora, heads * MLA_V)
            k_t = matmul(w_k_t, ckv, w_t=True, out_dtype=BF16)
            v = matmul(ckv, w_v, out_dtype=BF16)
            mixed = mla_attention(q, k_t, jnp.transpose(kr), v, lay=lay, heads=heads, with_ctx=ctx_out)
        rows = mixed.shape[0]
        y = matmul(mixed, ml_w_out if i % 2 == 0 else mla_w_out, layer=j)
        h, a = norm(h, layer=i, rows=rows, y=y, gate_idx=2, post_idx=1, pre_idx=2, shift_idx=3, scale_idx=4)
        u = matmul(a, ffn_w_up, layer=i)
        g = conv_gate(u, ffn_conv_w[i], ffn_conv_b[i], lay=lay)
        f = matmul(g, ffn_w_down, layer=i)
    out = norm(h, layer=depth - 1, rows=lay.lat_rows, y=f, gate_idx=5, post_idx=3)[0]
    return out.reshape(batch, seq, d)
```

```python
import functools
import math

import jax
import jax.numpy as jnp
import numpy as np
from jax import lax
from jax.experimental import pallas as pl
from jax.experimental.pallas import tpu as pltpu

F32 = jnp.float32
BF16 = jnp.bfloat16

LANE = 128
SUBLANE = 8
VMEM_LIMIT_BYTES = 56 * 1024 * 1024
W_BLOCK_BYTES = 8 * 1024 * 1024

EPS = 1e-6
ML_HEADS = 8
GATE_CAP = 15.0
GRID_W = 64
ROPE_THETA = 10000.0
MLA_NOPE = 128
MLA_ROPE = 64
MLA_V = 128
MLA_SCALE = (MLA_NOPE + MLA_ROPE) ** -0.5
MLA_QW = 2 * LANE
N_MOD = 6
MOD_ROWS = 8

ROW_TILE = 256
ATTN_Q_TILE = 2048
ATTN_SUB_TILE = 256
SCAN_HEADS_PER_STEP = 2


def _params(*sem):
    return pltpu.CompilerParams(dimension_semantics=sem, vmem_limit_bytes=VMEM_LIMIT_BYTES)


class Layout:
    def __init__(self, batch, seq, ctx_len):
        assert ctx_len == ROW_TILE and seq % (2 * ROW_TILE) == 0 and batch + 1 <= MOD_ROWS
        self.batch = batch
        self.seq_tiles = seq // ROW_TILE
        self.lat_tiles = batch * self.seq_tiles
        self.tiles = self.lat_tiles + batch
        self.lat_rows = batch * seq
        self.rows = self.tiles * ROW_TILE

    def mod_row(self, t):
        return jnp.where(t >= self.lat_tiles, 0, 1 + t // self.seq_tiles)

    def seq_first(self, t):
        return jnp.logical_or(t >= self.lat_tiles, t % self.seq_tiles == 0)

    def seq_last(self, t):
        return jnp.logical_or(t >= self.lat_tiles, t % self.seq_tiles == self.seq_tiles - 1)

    def ctx_tile(self, b):
        return self.lat_tiles + b


def _adaln_kernel(c_ref, w_ref, b_ref, o_ref):
    c = c_ref[...]
    s = c * jax.nn.sigmoid(c)
    o_ref[0] = jnp.dot(s, w_ref[0], preferred_element_type=F32) + b_ref[0]


def adaln_all(cond, mod_w, mod_b, tn=1024):
    depth, d, n = mod_w.shape
    return pl.pallas_call(
        _adaln_kernel,
        grid=(depth, n // tn),
        in_specs=[
            pl.BlockSpec((MOD_ROWS, d), lambda l, j: (0, 0)),
            pl.BlockSpec((1, d, tn), lambda l, j: (l, 0, j)),
            pl.BlockSpec((1, 1, tn), lambda l, j: (l, 0, j)),
        ],
        out_specs=pl.BlockSpec((1, MOD_ROWS, tn), lambda l, j: (l, 0, j)),
        out_shape=jax.ShapeDtypeStruct((depth, MOD_ROWS, n), F32),
        compiler_params=_params("arbitrary", "arbitrary"),
        name="adaln",
    )(cond, mod_w, mod_b.reshape(depth, 1, n))


def _rms(x, g):
    return x * lax.rsqrt(jnp.mean(x * x, axis=-1, keepdims=True) + EPS) * g


def _norm_kernel(*refs, has_y, has_a, split_tiles):
    it = iter(refs)
    if split_tiles is None:
        h = next(it)[...]
    else:
        lat_ref, ctx_ref = next(it), next(it)
        h = jnp.where(pl.program_id(0) < split_tiles, lat_ref[...], ctx_ref[...])
    if has_y:
        y_ref, gate_ref, gpost_ref = next(it), next(it), next(it)
    if has_a:
        gpre_ref, shift_ref, scale_ref = next(it), next(it), next(it)
    if has_y:
        hout_ref = next(it)
    if has_a:
        a_ref = next(it)
    if has_y:
        h = h + gate_ref[0] * _rms(y_ref[...], gpost_ref[0])
        hout_ref[...] = h
    if has_a:
        a = _rms(h, gpre_ref[0]) * (1.0 + scale_ref[0]) + shift_ref[0]
        a_ref[...] = a.astype(a_ref.dtype)


def norm_modulate(h, mods, norm_w, layer, *, lay, rows, y=None, y_layer=None, gate_idx=None, post_idx=None,
                  pre_idx=None, shift_idx=None, scale_idx=None):
    split = isinstance(h, tuple)
    d = h[0].shape[1] if split else h.shape[1]
    has_y, has_a = y is not None, pre_idx is not None
    y_layer = layer if y_layer is None else y_layer
    row_spec = pl.BlockSpec((ROW_TILE, d), lambda t: (t, 0))

    def mod_spec(lyr, which):
        return pl.BlockSpec((1, 1, d), lambda t: ((lyr * MOD_ROWS + lay.mod_row(t)) * N_MOD + which, 0, 0))

    def gain_spec(lyr, which):
        return pl.BlockSpec((1, 1, d), lambda t: (lyr * 4 + which, 0, 0))

    if split:
        args = list(h)
        in_specs = [pl.BlockSpec((ROW_TILE, d), lambda t: (jnp.minimum(t, lay.lat_tiles - 1), 0)),
                    pl.BlockSpec((ROW_TILE, d), lambda t: (jnp.maximum(t - lay.lat_tiles, 0), 0))]
    else:
        args, in_specs = [h], [row_spec]
    if has_y:
        args += [y, mods, norm_w]
        in_specs += [row_spec, mod_spec(y_layer, gate_idx), gain_spec(y_layer, post_idx)]
    if has_a:
        args += [norm_w, mods, mods]
        in_specs += [gain_spec(layer, pre_idx), mod_spec(layer, shift_idx), mod_spec(layer, scale_idx)]
    out_shape, out_specs = [], []
    if has_y:
        out_shape.append(jax.ShapeDtypeStruct((rows, d), F32))
        out_specs.append(row_spec)
    if has_a:
        out_shape.append(jax.ShapeDtypeStruct((rows, d), BF16))
        out_specs.append(row_spec)
    return pl.pallas_call(
        functools.partial(_norm_kernel, has_y=has_y, has_a=has_a, split_tiles=lay.lat_tiles if split else None),
        grid=(rows // ROW_TILE,),
        in_specs=in_specs,
        out_specs=out_specs,
        out_shape=out_shape,
        compiler_params=_params("arbitrary"),
        name="norm_modulate",
    )(*args)


def _rope_lanes(x, cos, sin):
    lane = lax.broadcasted_iota(jnp.int32, x.shape, 1)
    partner = jnp.where(lane % 32 < 16, pltpu.roll(x, LANE - 16, 1), pltpu.roll(x, 16, 1))
    return x * cos + partner * sin


def _pad_head_pairs(w):
    keep = lax.broadcasted_iota(jnp.int32, (w.shape[0], LANE), 1) < MLA_ROPE
    out = []
    for p in range(w.shape[1] // (3 * LANE)):
        nope0, mid, tail = (w[:, (3 * p + g) * LANE:(3 * p + g + 1) * LANE] for g in range(3))
        rolled = pltpu.roll(jnp.concatenate([mid, tail], axis=1), 2 * LANE - MLA_ROPE, 1)
        out += [nope0, jnp.where(keep, mid, 0.0), rolled[:, :LANE], jnp.where(keep, rolled[:, LANE:], 0.0)]
    return jnp.concatenate(out, axis=1)


def _mm_kernel(*refs, rope_scale, w_t, pad_heads):
    if rope_scale is not None:
        a_ref, w_ref, cos_ref, sin_ref, o_ref = refs
    else:
        a_ref, w_ref, o_ref = refs
    w = _pad_head_pairs(w_ref[...]) if pad_heads else w_ref[...]
    a, w = a_ref[...].astype(BF16), w.astype(BF16)
    if w_t:
        acc = lax.dot_general(a, w, (((1,), (1,)), ((), ())), preferred_element_type=F32)
    else:
        acc = jnp.dot(a, w, preferred_element_type=F32)
    if rope_scale is not None:
        cos, sin = cos_ref[...], sin_ref[...]
        for g in range(acc.shape[1] // LANE):
            blk = acc[:, g * LANE:(g + 1) * LANE]
            if g % 2 == 1:
                blk = _rope_lanes(blk, cos, sin)
            o_ref[:, g * LANE:(g + 1) * LANE] = (blk * rope_scale).astype(o_ref.dtype)
    else:
        o_ref[...] = acc.astype(o_ref.dtype)


def _pick_tm(r, cap):
    return max(tm for tm in range(16, cap + 1, 16) if r % tm == 0)


def _pick_tn(n, k):
    tn = 256
    while n % (2 * tn) == 0 and 2 * tn * k * 4 <= W_BLOCK_BYTES and 2 * tn <= 2048:
        tn *= 2
    return min(tn, n)


def matmul(a, w, *, layer=None, n_out=None, w_t=False, col0=0, tm_cap=1088, out_dtype=F32, rope=None,
           rope_scale=None, pad_heads=False):
    r, k = a.shape
    n_w, k_w = (w.shape[-2], w.shape[-1]) if w_t else (w.shape[-1], w.shape[-2])
    n_out = n_w if n_out is None else n_out
    tn = _pick_tn(n_out, k)
    assert n_out % tn == 0 and col0 % tn == 0 and k_w == k
    cb0 = col0 // tn
    tm = _pick_tm(r, tm_cap)
    if pad_heads:
        assert not w_t and tn % (4 * LANE) == 0 and n_w * 4 == n_out * 3
    w_block = (tn, k) if w_t else (k, tn * 3 // 4 if pad_heads else tn)
    w_index = (lambda j: (cb0 + j, 0)) if w_t else (lambda j: (0, cb0 + j))
    if layer is None:
        w_spec = pl.BlockSpec(w_block, lambda i, j: w_index(j))
    else:
        w_spec = pl.BlockSpec((None,) + w_block, lambda i, j: (layer,) + w_index(j))
    args = [a, w]
    in_specs = [pl.BlockSpec((tm, k), lambda i, j: (i, 0)), w_spec]
    if rope is not None:
        args += list(rope)
        in_specs += [pl.BlockSpec((tm, LANE), lambda i, j: (i, 0))] * 2
    return pl.pallas_call(
        functools.partial(_mm_kernel, rope_scale=rope_scale if rope is not None else None, w_t=w_t,
                          pad_heads=pad_heads),
        grid=(r // tm, n_out // tn),
        in_specs=in_specs,
        out_specs=pl.BlockSpec((tm, tn), lambda i, j: (i, j)),
        out_shape=jax.ShapeDtypeStruct((r, n_out), out_dtype),
        compiler_params=_params("arbitrary", "arbitrary"),
        name="matmul",
    )(*args)


def _col_to_row(col, eye):
    return jnp.sum(jnp.where(eye, col, 0.0), axis=0, keepdims=True)


def _mlstm_chunk(q, k, v, ig, lf, mask, eye, c_ref, n_ref, m_ref):
    neg_inf = -jnp.inf
    lf_row = _col_to_row(lf, eye)
    b = jnp.sum(jnp.where(mask, lf_row, 0.0), axis=1, keepdims=True)
    b_end = jnp.sum(lf, axis=0, keepdims=True)
    src = ig - b
    dmat = jnp.where(mask, b + _col_to_row(src, eye), neg_inf)
    m_prev = m_ref[...]
    inter = b + m_prev
    m_out = jnp.maximum(inter, jnp.max(dmat, axis=1, keepdims=True))
    w_intra = jnp.exp(dmat - m_out)
    w_inter = jnp.exp(inter - m_out)
    qk = lax.dot_general(q, k, (((1,), (1,)), ((), ())), preferred_element_type=F32)
    s = qk * w_intra
    c_prev = c_ref[...]
    num = jnp.dot(s.astype(BF16), v, preferred_element_type=F32)
    num = num + w_inter * jnp.dot(q, c_prev.astype(BF16), preferred_element_type=F32)
    qn = jnp.sum(q.astype(F32) * n_ref[...], axis=1, keepdims=True)
    den = jnp.sum(s, axis=1, keepdims=True) + w_inter * qn
    den = jnp.maximum(jnp.abs(den), jnp.exp(-m_out))
    h = num / den
    g = b_end + src
    m_new = jnp.maximum(b_end + m_prev, jnp.max(g, axis=0, keepdims=True))
    wg = jnp.exp(g - m_new)
    decay = jnp.exp(b_end + m_prev - m_new)
    kw = k.astype(F32) * wg
    c_ref[...] = decay * c_prev + lax.dot_general(
        kw.astype(BF16), v, (((0,), (0,)), ((), ())), preferred_element_type=F32)
    n_ref[...] = decay * n_ref[...] + jnp.sum(kw, axis=0, keepdims=True)
    m_ref[...] = m_new
    return h


def _mlstm_kernel(qf_ref, kf_ref, vf_ref, gf_ref, qb_ref, kb_ref, vb_ref, gb_ref, bias_ref,
                  hf_ref, hb_ref, cf_ref, nf_ref, mf_ref, cb_ref, nb_ref, mb_ref, *, dk_scale):
    heads_per_step = cf_ref.shape[0]
    dk, dv = cf_ref.shape[1], cf_ref.shape[2]

    @pl.when(pl.program_id(2) == 0)
    def _():
        for ref in (cf_ref, nf_ref, mf_ref, cb_ref, nb_ref, mb_ref):
            ref[...] = jnp.zeros_like(ref)

    chunk = qf_ref.shape[0]
    row = lax.broadcasted_iota(jnp.int32, (chunk, chunk), 0)
    col = lax.broadcasted_iota(jnp.int32, (chunk, chunk), 1)
    eye = row == col
    lane = lax.broadcasted_iota(jnp.int32, gf_ref.shape, 1)

    def gate_cols(g_ref, first, head):
        g = g_ref[...] + bias_ref[...]
        g = GATE_CAP * jnp.tanh(g / GATE_CAP)
        ig = jnp.sum(jnp.where(lane == first * ML_HEADS + head, g, 0.0), axis=1, keepdims=True)
        fg = jnp.sum(jnp.where(lane == (first + 1) * ML_HEADS + head, g, 0.0), axis=1, keepdims=True)
        return ig, jax.nn.log_sigmoid(fg)

    def run(i, q_ref, k_ref, v_ref, g_ref, first, mask, c_ref, n_ref, m_ref, h_ref):
        ig, lf = gate_cols(g_ref, first, pl.program_id(1) * heads_per_step + i)
        qk_cols, v_cols = slice(i * dk, (i + 1) * dk), slice(i * dv, (i + 1) * dv)
        q = q_ref[:, qk_cols] * dk_scale
        h_ref[:, v_cols] = _mlstm_chunk(q, k_ref[:, qk_cols], v_ref[:, v_cols], ig, lf, mask, eye,
                                        c_ref.at[i], n_ref.at[i], m_ref.at[i])

    for i in range(heads_per_step):
        run(i, qf_ref, kf_ref, vf_ref, gf_ref, 0, col <= row, cf_ref, nf_ref, mf_ref, hf_ref)
        run(i, qb_ref, kb_ref, vb_ref, gb_ref, 2, col >= row, cb_ref, nb_ref, mb_ref, hb_ref)


def mlstm_scan(qkv, gates, gate_bias, *, lay, dk, dv):
    r = qkv.shape[0]
    n_gates = gates.shape[1]
    st = lay.seq_tiles

    def fwd_tile(b, c):
        return jnp.where(c == 0, lay.ctx_tile(b), b * st + c - 1)

    def bwd_tile(b, c):
        return jnp.where(c == 0, lay.ctx_tile(b), b * st + st - c)

    hp = SCAN_HEADS_PER_STEP
    assert ML_HEADS % hp == 0

    def specs(tile):
        return [
            pl.BlockSpec((ROW_TILE, hp * dk), lambda b, h, c: (tile(b, c), h)),
            pl.BlockSpec((ROW_TILE, hp * dk), lambda b, h, c: (tile(b, c), ML_HEADS // hp + h)),
            pl.BlockSpec((ROW_TILE, hp * dv), lambda b, h, c: (tile(b, c), (2 * ML_HEADS * dk) // (hp * dv) + h)),
            pl.BlockSpec((ROW_TILE, n_gates), lambda b, h, c: (tile(b, c), 0)),
        ]

    def out_spec(tile):
        return pl.BlockSpec((ROW_TILE, hp * dv), lambda b, h, c: (tile(b, c), h))

    state = [pltpu.VMEM((hp, dk, dv), F32), pltpu.VMEM((hp, 1, dk), F32), pltpu.VMEM((hp, 1, 1), F32)]
    return pl.pallas_call(
        functools.partial(_mlstm_kernel, dk_scale=dk ** -0.5),
        grid=(lay.batch, ML_HEADS // hp, st + 1),
        in_specs=specs(fwd_tile) + specs(bwd_tile) + [pl.BlockSpec((1, n_gates), lambda b, h, c: (0, 0))],
        out_specs=[out_spec(fwd_tile), out_spec(bwd_tile)],
        out_shape=[jax.ShapeDtypeStruct((r, ML_HEADS * dv), F32)] * 2,
        scratch_shapes=state + state,
        compiler_params=_params("arbitrary", "arbitrary", "arbitrary"),
        name="mlstm_scan",
    )(qkv, qkv, qkv, gates, qkv, qkv, qkv, gates, gate_bias)


def _readout_kernel(hf_ref, hb_ref, o_ref, g_ref, out_ref, *, dv):
    for hd in range(ML_HEADS):
        sl = slice(hd * dv, (hd + 1) * dv)
        hs = hf_ref[:, sl] + hb_ref[:, sl]
        o = o_ref[:, sl]
        out_ref[:, sl] = (jax.nn.sigmoid(o) * _rms(hs, g_ref[:, sl])).astype(out_ref.dtype)


def mlstm_readout(h_f, h_b, o, g_norm, *, dv):
    r, width = h_f.shape
    row = pl.BlockSpec((ROW_TILE, width), lambda t: (t, 0))
    return pl.pallas_call(
        functools.partial(_readout_kernel, dv=dv),
        grid=(r // ROW_TILE,),
        in_specs=[row, row, row, pl.BlockSpec((1, width), lambda t: (0, 0))],
        out_specs=row,
        out_shape=jax.ShapeDtypeStruct((r, width), BF16),
        compiler_params=_params("arbitrary"),
        name="mlstm_readout",
    )(h_f, h_b, o, g_norm)


def _mla_prep_kernel(down_ref, krin_ref, qn_ref, kvn_ref, cos_ref, sin_ref, cq_ref, ckv_ref, kr_ref, *,
                     q_lora, kv_lora):
    cq_ref[...] = _rms(down_ref[:, :q_lora], qn_ref[...]).astype(cq_ref.dtype)
    ckv_ref[...] = _rms(down_ref[:, q_lora:q_lora + kv_lora], kvn_ref[...]).astype(ckv_ref.dtype)
    kr = krin_ref[...]
    lane = lax.broadcasted_iota(jnp.int32, kr.shape, 1)
    kr = jnp.where(lane < MLA_ROPE, kr, 0.0)
    kr_ref[...] = _rope_lanes(kr, cos_ref[...], sin_ref[...]).astype(kr_ref.dtype)


def mla_prep(down, kr_raw, q_norm, kv_norm, cos, sin, *, q_lora, kv_lora):
    r, width = down.shape
    return pl.pallas_call(
        functools.partial(_mla_prep_kernel, q_lora=q_lora, kv_lora=kv_lora),
        grid=(r // ROW_TILE,),
        in_specs=[pl.BlockSpec((ROW_TILE, width), lambda t: (t, 0)),
                  pl.BlockSpec((ROW_TILE, LANE), lambda t: (t, 0)),
                  pl.BlockSpec((1, q_lora), lambda t: (0, 0)),
                  pl.BlockSpec((1, kv_lora), lambda t: (0, 0)),
                  pl.BlockSpec((ROW_TILE, LANE), lambda t: (t, 0)),
                  pl.BlockSpec((ROW_TILE, LANE), lambda t: (t, 0))],
        out_specs=[pl.BlockSpec((ROW_TILE, q_lora), lambda t: (t, 0)),
                   pl.BlockSpec((ROW_TILE, kv_lora), lambda t: (t, 0)),
                   pl.BlockSpec((ROW_TILE, LANE), lambda t: (t, 0))],
        out_shape=[jax.ShapeDtypeStruct((r, q_lora), BF16),
                   jax.ShapeDtypeStruct((r, kv_lora), BF16),
                   jax.ShapeDtypeStruct((r, LANE), BF16)],
        compiler_params=_params("arbitrary"),
        name="mla_prep",
    )(down, kr_raw, q_norm, kv_norm, cos, sin)


def _softmax_attend(q, k_t, v):
    s = jnp.dot(q, k_t, preferred_element_type=F32)
    p = jnp.exp2(s - jnp.max(s, axis=1, keepdims=True))
    o = jnp.dot(p.astype(BF16), v, preferred_element_type=F32)
    return o / jnp.sum(p, axis=1, keepdims=True)


def _attn_lat_kernel(q_ref, knl_ref, krl_ref, vl_ref, knc_ref, krc_ref, vc_ref, o_ref, kcat_ref, vcat_ref):
    ctx_len = vc_ref.shape[0]

    @pl.when(pl.program_id(2) == 0)
    def _():
        kcat_ref[:MLA_NOPE, :ctx_len] = knc_ref[...]
        kcat_ref[MLA_NOPE:, :ctx_len] = krc_ref[...]
        kcat_ref[:MLA_NOPE, ctx_len:] = knl_ref[...]
        kcat_ref[MLA_NOPE:, ctx_len:] = krl_ref[...]
        vcat_ref[:ctx_len, :] = vc_ref[...]
        vcat_ref[ctx_len:, :] = vl_ref[...]

    k_t, v = kcat_ref[...], vcat_ref[...]
    sub = min(ATTN_SUB_TILE, q_ref.shape[0])
    for part in range(q_ref.shape[0] // sub):
        rows = slice(part * sub, (part + 1) * sub)
        o_ref[rows, :] = _softmax_attend(q_ref[rows, :], k_t, v).astype(o_ref.dtype)


def _attn_ctx_kernel(q_ref, kn_ref, kr_ref, v_ref, prev_ref, o_ref):
    del prev_ref
    k_t = jnp.concatenate([kn_ref[...], kr_ref[...]], axis=0)
    o_ref[...] = _softmax_attend(q_ref[...], k_t, v_ref[...]).astype(o_ref.dtype)


def mla_attention(q, k_t, kr_t, v, *, lay, heads, with_ctx):
    seq = lay.seq_tiles * ROW_TILE
    tq = min(ATTN_Q_TILE, seq)
    q_tiles = seq // tq
    out_rows = lay.rows if with_ctx else lay.lat_rows

    def keys_lat(row):
        return pl.BlockSpec((MLA_NOPE, seq), lambda b, h, j: (row(h), b))

    def keys_ctx(row):
        return pl.BlockSpec((MLA_NOPE, ROW_TILE), lambda b, h, j: (row(h), lay.ctx_tile(b)))

    own, shared = (lambda h: h), (lambda h: 0)
    att = pl.pallas_call(
        _attn_lat_kernel,
        grid=(lay.batch, heads, q_tiles),
        in_specs=[pl.BlockSpec((tq, MLA_QW), lambda b, h, j: (b * q_tiles + j, h)),
                  keys_lat(own), keys_lat(shared), pl.BlockSpec((seq, MLA_V), lambda b, h, j: (b, h)),
                  keys_ctx(own), keys_ctx(shared),
                  pl.BlockSpec((ROW_TILE, MLA_V), lambda b, h, j: (lay.ctx_tile(b), h))],
        out_specs=pl.BlockSpec((tq, MLA_V), lambda b, h, j: (b * q_tiles + j, h)),
        out_shape=jax.ShapeDtypeStruct((out_rows, heads * MLA_V), BF16),
        scratch_shapes=[pltpu.VMEM((MLA_QW, ROW_TILE + seq), BF16), pltpu.VMEM((ROW_TILE + seq, MLA_V), BF16)],
        compiler_params=_params("arbitrary", "arbitrary", "arbitrary"),
        name="mla_attention",
    )(q, k_t, kr_t, v, k_t, kr_t, v)
    if not with_ctx:
        return att

    def ctx_keys(row):
        return pl.BlockSpec((MLA_NOPE, ROW_TILE), lambda b, h: (row(h), lay.ctx_tile(b)))

    def ctx_rows(width):
        return pl.BlockSpec((ROW_TILE, width), lambda b, h: (lay.ctx_tile(b), h))

    return pl.pallas_call(
        _attn_ctx_kernel,
        grid=(lay.batch, heads),
        in_specs=[ctx_rows(MLA_QW), ctx_keys(own), ctx_keys(shared), ctx_rows(MLA_V),
                  pl.BlockSpec(memory_space=pl.ANY)],
        out_specs=ctx_rows(MLA_V),
        out_shape=jax.ShapeDtypeStruct(att.shape, att.dtype),
        input_output_aliases={4: 0},
        compiler_params=_params("arbitrary", "arbitrary"),
        name="mla_attention_ctx",
    )(q, k_t, kr_t, v, att)


def _conv_gate_kernel(ug_ref, uv_ref, pg_ref, pv_ref, ng_ref, nv_ref, wg_ref, wv_ref, bg_ref, bv_ref, o_ref, *, lay):
    t = pl.program_id(0)
    first, last = lay.seq_first(t), lay.seq_last(t)
    rows = ug_ref.shape[0]
    ridx = lax.broadcasted_iota(jnp.int32, ug_ref.shape, 0)

    def conv(u_ref, p_ref, n_ref, w_ref, b_ref):
        u = u_ref[...]
        prev_row = jnp.where(first, 0.0, p_ref[SUBLANE - 1:SUBLANE, :])
        next_row = jnp.where(last, 0.0, n_ref[0:1, :])
        up = jnp.where(ridx == 0, prev_row, pltpu.roll(u, 1, 0))
        un = jnp.where(ridx == rows - 1, next_row, pltpu.roll(u, rows - 1, 0))
        return up * w_ref[0:1, :] + u * w_ref[1:2, :] + un * w_ref[2:3, :] + b_ref[...]

    gate = conv(ug_ref, pg_ref, ng_ref, wg_ref, bg_ref)
    val = conv(uv_ref, pv_ref, nv_ref, wv_ref, bv_ref)
    o_ref[...] = (gate * jax.nn.sigmoid(gate) * val).astype(o_ref.dtype)


def conv_gate(u, conv_w, conv_b, *, lay, tn=1024):
    r, two_f = u.shape
    f = two_f // 2
    nb = f // tn
    per_tile = ROW_TILE // SUBLANE
    last_sub = r // SUBLANE - 1

    def main(off):
        return pl.BlockSpec((ROW_TILE, tn), lambda t, j: (t, off + j))

    def prev(off):
        return pl.BlockSpec((SUBLANE, tn), lambda t, j: (jnp.maximum(t * per_tile - 1, 0), off + j))

    def nxt(off):
        return pl.BlockSpec((SUBLANE, tn), lambda t, j: (jnp.minimum((t + 1) * per_tile, last_sub), off + j))

    def wspec(off):
        return pl.BlockSpec((3, tn), lambda t, j: (0, off + j))

    def bspec(off):
        return pl.BlockSpec((1, tn), lambda t, j: (0, off + j))

    return pl.pallas_call(
        functools.partial(_conv_gate_kernel, lay=lay),
        grid=(r // ROW_TILE, nb),
        in_specs=[main(0), main(nb), prev(0), prev(nb), nxt(0), nxt(nb), wspec(0), wspec(nb), bspec(0), bspec(nb)],
        out_specs=pl.BlockSpec((ROW_TILE, tn), lambda t, j: (t, j)),
        out_shape=jax.ShapeDtypeStruct((r, f), BF16),
        compiler_params=_params("arbitrary", "arbitrary"),
        name="conv_gate",
    )(u, u, u, u, u, u, conv_w, conv_w, conv_b.reshape(1, two_f), conv_b.reshape(1, two_f))


def _rope_tables(batch, ctx_len, seq):
    per_axis = MLA_ROPE // 2
    inv_freq = np.float32(ROPE_THETA) ** (-np.arange(0, per_axis, 2, dtype=np.float32) / np.float32(per_axis))
    pos = np.arange(seq)
    ang_r = (pos // GRID_W).astype(np.float32)[:, None] * inv_freq
    ang_c = (pos % GRID_W).astype(np.float32)[:, None] * inv_freq
    cos64 = np.concatenate([np.cos(ang_r)] * 2 + [np.cos(ang_c)] * 2, axis=1)
    sin64 = np.concatenate([-np.sin(ang_r), np.sin(ang_r), -np.sin(ang_c), np.sin(ang_c)], axis=1)
    cos_lat = np.tile(np.concatenate([cos64, cos64], axis=1), (batch, 1))
    sin_lat = np.tile(np.concatenate([sin64, sin64], axis=1), (batch, 1))
    cos = np.concatenate([cos_lat, np.ones((batch * ctx_len, LANE), np.float32)], axis=0)
    sin = np.concatenate([sin_lat, np.zeros((batch * ctx_len, LANE), np.float32)], axis=0)
    return jnp.asarray(cos, F32), jnp.asarray(sin, F32)


def kernel(x, c, ctx, c_ctx, mod_w, mod_b, norm_w, ml_w_in, ml_b_gate, ml_norm, ml_w_out, mla_w_down, mla_q_norm,
           mla_w_uq, mla_kv_norm, mla_w_ukv, mla_w_out, ffn_w_up, ffn_conv_w, ffn_conv_b, ffn_w_down):
    batch, seq, d = x.shape
    ctx_len = ctx.shape[1]
    depth = mod_w.shape[0]
    lay = Layout(batch, seq, ctx_len)
    ml_qk = d // 2
    dk, dv = ml_qk // ML_HEADS, d // ML_HEADS
    heads = d // MLA_NOPE
    q_lora, kv_lora = d // 4, d // 8

    h = (x.reshape(batch * seq, d), ctx.reshape(batch * ctx_len, d))
    cond = jnp.concatenate([c_ctx[None], c, jnp.zeros((MOD_ROWS - 1 - batch, d), F32)], axis=0)
    mods = adaln_all(cond, mod_w, mod_b).reshape(depth * MOD_ROWS * N_MOD, 1, d)
    gains = norm_w.reshape(depth * 4, 1, d)
    cos, sin = _rope_tables(batch, ctx_len, seq)
    norm = functools.partial(norm_modulate, mods=mods, norm_w=gains, lay=lay)
    w_in_t = jnp.swapaxes(ml_w_in, 1, 2)
    w_down_t = jnp.swapaxes(mla_w_down, 1, 2)

    f = None
    for i in range(depth):
        j = i // 2
        ctx_out = i < depth - 1
        if f is None:
            a = norm(h, layer=i, rows=lay.rows, pre_idx=0, shift_idx=0, scale_idx=1)[0]
        else:
            h, a = norm(h, layer=i, rows=lay.rows, y=f, y_layer=i - 1, gate_idx=5, post_idx=3,
                        pre_idx=0, shift_idx=0, scale_idx=1)
        if i % 2 == 0:
            qkv = matmul(a, w_in_t, layer=j, w_t=True, n_out=2 * ml_qk + d, out_dtype=BF16)
            o = matmul(a, w_in_t, layer=j, w_t=True, n_out=d, col0=2 * ml_qk + d)
            gates = matmul(a, w_in_t, layer=j, w_t=True, n_out=4 * ML_HEADS, col0=2 * ml_qk + 2 * d)
            bias = ml_b_gate[j].reshape(1, 4 * ML_HEADS)
            h_f, h_b = mlstm_scan(qkv, gates, bias, lay=lay, dk=dk, dv=dv)
            mixed = mlstm_readout(h_f, h_b, o, ml_norm[j].reshape(1, d), dv=dv)
        else:
            down = matmul(a, w_down_t, layer=j, w_t=True, n_out=q_lora + kv_lora)
            kr_raw = matmul(a, w_down_t, layer=j, w_t=True, n_out=LANE, col0=q_lora + kv_lora)
            cq, ckv, kr = mla_prep(down, kr_raw, mla_q_norm[j].reshape(1, q_lora),
                                   mla_kv_norm[j].reshape(1, kv_lora), cos, sin, q_lora=q_lora, kv_lora=kv_lora)
            q = matmul(cq, mla_w_uq, layer=j, pad_heads=True, n_out=heads * MLA_QW, out_dtype=BF16,
                       rope=(cos, sin), rope_scale=MLA_SCALE * math.log2(math.e))
            w_kv = mla_w_ukv[j].reshape(kv_lora, heads, MLA_NOPE + MLA_V)
            w_k_t = jnp.transpose(w_kv[:, :, :MLA_NOPE].reshape(kv_lora, heads * MLA_NOPE))
            w_v = w_kv[:, :, MLA_NOPE:].reshape(kv_lora, heads * MLA_V)
            k_t = matmul(w_k_t, ckv, w_t=True, out_dtype=BF16)
            v = matmul(ckv, w_v, out_dtype=BF16)
            mixed = mla_attention(q, k_t, jnp.transpose(kr), v, lay=lay, heads=heads, with_ctx=ctx_out)
        rows = mixed.shape[0]
        y = matmul(mixed, ml_w_out if i % 2 == 0 else mla_w_out, layer=j)
        h, a = norm(h, layer=i, rows=rows, y=y, gate_idx=2, post_idx=1, pre_idx=2, shift_idx=3, scale_idx=4)
        u = matmul(a, ffn_w_up, layer=i)
        g = conv_gate(u, ffn_conv_w[i], ffn_conv_b[i], lay=lay)
        f = matmul(g, ffn_w_down, layer=i)
    out = norm(h, layer=depth - 1, rows=lay.lat_rows, y=f, gate_idx=5, post_idx=3)[0]
    return out.reshape(batch, seq, d)
```

```python
import functools
import math

import jax
import jax.numpy as jnp
import numpy as np
from jax import lax
from jax.experimental import pallas as pl
from jax.experimental.pallas import tpu as pltpu

F32 = jnp.float32
BF16 = jnp.bfloat16

LANE = 128
SUBLANE = 8
VMEM_LIMIT_BYTES = 56 * 1024 * 1024
VMEM_BLOCK_FRACTION = 0.9

EPS = 1e-6
ML_HEADS = 8
GATE_CAP = 15.0
GRID_W = 64
ROPE_THETA = 10000.0
MLA_NOPE = 128
MLA_ROPE = 64
MLA_V = 128
MLA_SCALE = (MLA_NOPE + MLA_ROPE) ** -0.5
MLA_QW = 2 * LANE
N_MOD = 6
MOD_ROWS = 8

ROW_TILE = 256
ATTN_Q_TILE = 2048
ATTN_SUB_TILE = 256
SCAN_HEADS_PER_STEP = 2


def _params(*sem):
    return pltpu.CompilerParams(dimension_semantics=sem, vmem_limit_bytes=VMEM_LIMIT_BYTES)


class Layout:
    def __init__(self, batch, seq, ctx_len):
        assert ctx_len == ROW_TILE and seq % (2 * ROW_TILE) == 0 and batch + 1 <= MOD_ROWS
        self.batch = batch
        self.seq_tiles = seq // ROW_TILE
        self.lat_tiles = batch * self.seq_tiles
        self.tiles = self.lat_tiles + batch
        self.lat_rows = batch * seq
        self.rows = self.tiles * ROW_TILE

    def mod_row(self, t):
        return jnp.where(t >= self.lat_tiles, 0, 1 + t // self.seq_tiles)

    def seq_first(self, t):
        return jnp.logical_or(t >= self.lat_tiles, t % self.seq_tiles == 0)

    def seq_last(self, t):
        return jnp.logical_or(t >= self.lat_tiles, t % self.seq_tiles == self.seq_tiles - 1)

    def ctx_tile(self, b):
        return self.lat_tiles + b


def _adaln_kernel(c_ref, w_ref, b_ref, o_ref):
    c = c_ref[...]
    s = c * jax.nn.sigmoid(c)
    o_ref[0] = jnp.dot(s, w_ref[0], preferred_element_type=F32) + b_ref[0]


def adaln_all(cond, mod_w, mod_b, tn=1024):
    depth, d, n = mod_w.shape
    return pl.pallas_call(
        _adaln_kernel,
        grid=(depth, n // tn),
        in_specs=[
            pl.BlockSpec((MOD_ROWS, d), lambda l, j: (0, 0)),
            pl.BlockSpec((1, d, tn), lambda l, j: (l, 0, j)),
            pl.BlockSpec((1, 1, tn), lambda l, j: (l, 0, j)),
        ],
        out_specs=pl.BlockSpec((1, MOD_ROWS, tn), lambda l, j: (l, 0, j)),
        out_shape=jax.ShapeDtypeStruct((depth, MOD_ROWS, n), F32),
        compiler_params=_params("arbitrary", "arbitrary"),
        name="adaln",
    )(cond, mod_w, mod_b.reshape(depth, 1, n))


def _rms(x, g):
    return x * lax.rsqrt(jnp.mean(x * x, axis=-1, keepdims=True) + EPS) * g


def _norm_kernel(*refs, has_y, has_a, split_tiles):
    it = iter(refs)
    if split_tiles is None:
        h = next(it)[...]
    else:
        lat_ref, ctx_ref = next(it), next(it)
        h = jnp.where(pl.program_id(0) < split_tiles, lat_ref[...], ctx_ref[...])
    if has_y:
        y_ref, gate_ref, gpost_ref = next(it), next(it), next(it)
    if has_a:
        gpre_ref, shift_ref, scale_ref = next(it), next(it), next(it)
    if has_y:
        hout_ref = next(it)
    if has_a:
        a_ref = next(it)
    if has_y:
        h = h + gate_ref[0] * _rms(y_ref[...], gpost_ref[0])
        hout_ref[...] = h
    if has_a:
        a = _rms(h, gpre_ref[0]) * (1.0 + scale_ref[0]) + shift_ref[0]
        a_ref[...] = a.astype(a_ref.dtype)


def norm_modulate(h, mods, norm_w, layer, *, lay, rows, y=None, y_layer=None, gate_idx=None, post_idx=None,
                  pre_idx=None, shift_idx=None, scale_idx=None):
    split = isinstance(h, tuple)
    d = h[0].shape[1] if split else h.shape[1]
    has_y, has_a = y is not None, pre_idx is not None
    y_layer = layer if y_layer is None else y_layer
    row_spec = pl.BlockSpec((ROW_TILE, d), lambda t: (t, 0))

    def mod_spec(lyr, which):
        return pl.BlockSpec((1, 1, d), lambda t: ((lyr * MOD_ROWS + lay.mod_row(t)) * N_MOD + which, 0, 0))

    def gain_spec(lyr, which):
        return pl.BlockSpec((1, 1, d), lambda t: (lyr * 4 + which, 0, 0))

    if split:
        args = list(h)
        in_specs = [pl.BlockSpec((ROW_TILE, d), lambda t: (jnp.minimum(t, lay.lat_tiles - 1), 0)),
                    pl.BlockSpec((ROW_TILE, d), lambda t: (jnp.maximum(t - lay.lat_tiles, 0), 0))]
    else:
        args, in_specs = [h], [row_spec]
    if has_y:
        args += [y, mods, norm_w]
        in_specs += [row_spec, mod_spec(y_layer, gate_idx), gain_spec(y_layer, post_idx)]
    if has_a:
        args += [norm_w, mods, mods]
        in_specs += [gain_spec(layer, pre_idx), mod_spec(layer, shift_idx), mod_spec(layer, scale_idx)]
    out_shape, out_specs = [], []
    if has_y:
        out_shape.append(jax.ShapeDtypeStruct((rows, d), F32))
        out_specs.append(row_spec)
    if has_a:
        out_shape.append(jax.ShapeDtypeStruct((rows, d), BF16))
        out_specs.append(row_spec)
    return pl.pallas_call(
        functools.partial(_norm_kernel, has_y=has_y, has_a=has_a, split_tiles=lay.lat_tiles if split else None),
        grid=(rows // ROW_TILE,),
        in_specs=in_specs,
        out_specs=out_specs,
        out_shape=out_shape,
        compiler_params=_params("arbitrary"),
        name="norm_modulate",
    )(*args)


def _rope_lanes(x, cos, sin):
    lane = lax.broadcasted_iota(jnp.int32, x.shape, 1)
    partner = jnp.where(lane % 32 < 16, pltpu.roll(x, LANE - 16, 1), pltpu.roll(x, 16, 1))
    return x * cos + partner * sin


def _pad_head_pairs(w):
    keep = lax.broadcasted_iota(jnp.int32, (w.shape[0], LANE), 1) < MLA_ROPE
    out = []
    for p in range(w.shape[1] // (3 * LANE)):
        nope0, mid, tail = (w[:, (3 * p + g) * LANE:(3 * p + g + 1) * LANE] for g in range(3))
        rolled = pltpu.roll(jnp.concatenate([mid, tail], axis=1), 2 * LANE - MLA_ROPE, 1)
        out += [nope0, jnp.where(keep, mid, 0.0), rolled[:, :LANE], jnp.where(keep, rolled[:, LANE:], 0.0)]
    return jnp.concatenate(out, axis=1)


def _mm_kernel(*refs, rope_scale, w_t, pad_heads):
    if rope_scale is not None:
        a_ref, w_ref, cos_ref, sin_ref, o_ref = refs
    else:
        a_ref, w_ref, o_ref = refs
    w = _pad_head_pairs(w_ref[...]) if pad_heads else w_ref[...]
    a, w = a_ref[...].astype(BF16), w.astype(BF16)
    if w_t:
        acc = lax.dot_general(a, w, (((1,), (1,)), ((), ())), preferred_element_type=F32)
    else:
        acc = jnp.dot(a, w, preferred_element_type=F32)
    if rope_scale is not None:
        cos, sin = cos_ref[...], sin_ref[...]
        for g in range(acc.shape[1] // LANE):
            blk = acc[:, g * LANE:(g + 1) * LANE]
            if g % 2 == 1:
                blk = _rope_lanes(blk, cos, sin)
            o_ref[:, g * LANE:(g + 1) * LANE] = (blk * rope_scale).astype(o_ref.dtype)
    else:
        o_ref[...] = acc.astype(o_ref.dtype)


def _pick_blocks(r, k, n_out, a_itemsize, out_itemsize, tm_cap):
    tm = max(t for t in range(16, tm_cap + 1, 16) if r % t == 0)
    budget = VMEM_BLOCK_FRACTION * VMEM_LIMIT_BYTES
    candidates = [t for t in (2048, 1024, 512, 256) if n_out % t == 0] or [n_out]
    for tn in candidates:
        fixed = k * tn * (2 * 4 + 2) + 2 * tm * tn * out_itemsize
        for a_copies in (2, 1):
            if fixed + a_copies * tm * k * a_itemsize <= budget:
                return tm, tn, a_copies == 1
    raise ValueError("no matmul tiling fits VMEM")


def matmul(a, w, *, layer=None, n_out=None, w_t=False, col0=0, tm_cap=1088, out_dtype=F32, rope=None,
           rope_scale=None, pad_heads=False):
    r, k = a.shape
    n_w, k_w = (w.shape[-2], w.shape[-1]) if w_t else (w.shape[-1], w.shape[-2])
    n_out = n_w if n_out is None else n_out
    tm, tn, single_a = _pick_blocks(r, k, n_out, a.dtype.itemsize, jnp.dtype(out_dtype).itemsize, tm_cap)
    assert n_out % tn == 0 and col0 % tn == 0 and k_w == k
    cb0 = col0 // tn
    if pad_heads:
        assert not w_t and tn % (4 * LANE) == 0 and n_w * 4 == n_out * 3
    w_block = (tn, k) if w_t else (k, tn * 3 // 4 if pad_heads else tn)
    w_index = (lambda j: (cb0 + j, 0)) if w_t else (lambda j: (0, cb0 + j))
    if layer is None:
        w_spec = pl.BlockSpec(w_block, lambda i, j: w_index(j))
    else:
        w_spec = pl.BlockSpec((None,) + w_block, lambda i, j: (layer,) + w_index(j))
    args = [a, w]
    a_mode = {"pipeline_mode": pl.Buffered(1)} if single_a else {}
    in_specs = [pl.BlockSpec((tm, k), lambda i, j: (i, 0), **a_mode), w_spec]
    if rope is not None:
        args += list(rope)
        in_specs += [pl.BlockSpec((tm, LANE), lambda i, j: (i, 0))] * 2
    return pl.pallas_call(
        functools.partial(_mm_kernel, rope_scale=rope_scale if rope is not None else None, w_t=w_t,
                          pad_heads=pad_heads),
        grid=(r // tm, n_out // tn),
        in_specs=in_specs,
        out_specs=pl.BlockSpec((tm, tn), lambda i, j: (i, j)),
        out_shape=jax.ShapeDtypeStruct((r, n_out), out_dtype),
        compiler_params=_params("arbitrary", "arbitrary"),
        name="matmul",
    )(*args)


def _col_to_row(col, eye):
    return jnp.sum(jnp.where(eye, col, 0.0), axis=0, keepdims=True)


def _mlstm_chunk(q, k, v, ig, lf, mask, eye, c_ref, n_ref, m_ref):
    neg_inf = -jnp.inf
    lf_row = _col_to_row(lf, eye)
    b = jnp.sum(jnp.where(mask, lf_row, 0.0), axis=1, keepdims=True)
    b_end = jnp.sum(lf, axis=0, keepdims=True)
    src = ig - b
    dmat = jnp.where(mask, b + _col_to_row(src, eye), neg_inf)
    m_prev = m_ref[...]
    inter = b + m_prev
    m_out = jnp.maximum(inter, jnp.max(dmat, axis=1, keepdims=True))
    w_intra = jnp.exp(dmat - m_out)
    w_inter = jnp.exp(inter - m_out)
    qk = lax.dot_general(q, k, (((1,), (1,)), ((), ())), preferred_element_type=F32)
    s = qk * w_intra
    c_prev = c_ref[...]
    num = jnp.dot(s.astype(BF16), v, preferred_element_type=F32)
    num = num + w_inter * jnp.dot(q, c_prev.astype(BF16), preferred_element_type=F32)
    qn = jnp.sum(q.astype(F32) * n_ref[...], axis=1, keepdims=True)
    den = jnp.sum(s, axis=1, keepdims=True) + w_inter * qn
    den = jnp.maximum(jnp.abs(den), jnp.exp(-m_out))
    h = num / den
    g = b_end + src
    m_new = jnp.maximum(b_end + m_prev, jnp.max(g, axis=0, keepdims=True))
    wg = jnp.exp(g - m_new)
    decay = jnp.exp(b_end + m_prev - m_new)
    kw = k.astype(F32) * wg
    c_ref[...] = decay * c_prev + lax.dot_general(
        kw.astype(BF16), v, (((0,), (0,)), ((), ())), preferred_element_type=F32)
    n_ref[...] = decay * n_ref[...] + jnp.sum(kw, axis=0, keepdims=True)
    m_ref[...] = m_new
    return h


def _mlstm_kernel(qf_ref, kf_ref, vf_ref, gf_ref, qb_ref, kb_ref, vb_ref, gb_ref, bias_ref,
                  hf_ref, hb_ref, cf_ref, nf_ref, mf_ref, cb_ref, nb_ref, mb_ref, *, dk_scale):
    heads_per_step = cf_ref.shape[0]
    dk, dv = cf_ref.shape[1], cf_ref.shape[2]

    @pl.when(pl.program_id(2) == 0)
    def _():
        for ref in (cf_ref, nf_ref, mf_ref, cb_ref, nb_ref, mb_ref):
            ref[...] = jnp.zeros_like(ref)

    chunk = qf_ref.shape[0]
    row = lax.broadcasted_iota(jnp.int32, (chunk, chunk), 0)
    col = lax.broadcasted_iota(jnp.int32, (chunk, chunk), 1)
    eye = row == col
    lane = lax.broadcasted_iota(jnp.int32, gf_ref.shape, 1)

    def gate_cols(g_ref, first, head):
        g = g_ref[...] + bias_ref[...]
        g = GATE_CAP * jnp.tanh(g / GATE_CAP)
        ig = jnp.sum(jnp.where(lane == first * ML_HEADS + head, g, 0.0), axis=1, keepdims=True)
        fg = jnp.sum(jnp.where(lane == (first + 1) * ML_HEADS + head, g, 0.0), axis=1, keepdims=True)
        return ig, jax.nn.log_sigmoid(fg)

    def run(i, q_ref, k_ref, v_ref, g_ref, first, mask, c_ref, n_ref, m_ref, h_ref):
        ig, lf = gate_cols(g_ref, first, pl.program_id(1) * heads_per_step + i)
        qk_cols, v_cols = slice(i * dk, (i + 1) * dk), slice(i * dv, (i + 1) * dv)
        q = q_ref[:, qk_cols] * dk_scale
        h_ref[:, v_cols] = _mlstm_chunk(q, k_ref[:, qk_cols], v_ref[:, v_cols], ig, lf, mask, eye,
                                        c_ref.at[i], n_ref.at[i], m_ref.at[i])

    for i in range(heads_per_step):
        run(i, qf_ref, kf_ref, vf_ref, gf_ref, 0, col <= row, cf_ref, nf_ref, mf_ref, hf_ref)
        run(i, qb_ref, kb_ref, vb_ref, gb_ref, 2, col >= row, cb_ref, nb_ref, mb_ref, hb_ref)


def mlstm_scan(qkv, gates, gate_bias, *, lay, dk, dv):
    r = qkv.shape[0]
    n_gates = gates.shape[1]
    st = lay.seq_tiles

    def fwd_tile(b, c):
        return jnp.where(c == 0, lay.ctx_tile(b), b * st + c - 1)

    def bwd_tile(b, c):
        return jnp.where(c == 0, lay.ctx_tile(b), b * st + st - c)

    hp = SCAN_HEADS_PER_STEP
    assert ML_HEADS % hp == 0

    def specs(tile):
        return [
            pl.BlockSpec((ROW_TILE, hp * dk), lambda b, h, c: (tile(b, c), h)),
            pl.BlockSpec((ROW_TILE, hp * dk), lambda b, h, c: (tile(b, c), ML_HEADS // hp + h)),
            pl.BlockSpec((ROW_TILE, hp * dv), lambda b, h, c: (tile(b, c), (2 * ML_HEADS * dk) // (hp * dv) + h)),
            pl.BlockSpec((ROW_TILE, n_gates), lambda b, h, c: (tile(b, c), 0)),
        ]

    def out_spec(tile):
        return pl.BlockSpec((ROW_TILE, hp * dv), lambda b, h, c: (tile(b, c), h))

    state = [pltpu.VMEM((hp, dk, dv), F32), pltpu.VMEM((hp, 1, dk), F32), pltpu.VMEM((hp, 1, 1), F32)]
    return pl.pallas_call(
        functools.partial(_mlstm_kernel, dk_scale=dk ** -0.5),
        grid=(lay.batch, ML_HEADS // hp, st + 1),
        in_specs=specs(fwd_tile) + specs(bwd_tile) + [pl.BlockSpec((1, n_gates), lambda b, h, c: (0, 0))],
        out_specs=[out_spec(fwd_tile), out_spec(bwd_tile)],
        out_shape=[jax.ShapeDtypeStruct((r, ML_HEADS * dv), F32)] * 2,
        scratch_shapes=state + state,
        compiler_params=_params("arbitrary", "arbitrary", "arbitrary"),
        name="mlstm_scan",
    )(qkv, qkv, qkv, gates, qkv, qkv, qkv, gates, gate_bias)


def _readout_kernel(hf_ref, hb_ref, o_ref, g_ref, out_ref, *, dv):
    for hd in range(ML_HEADS):
        sl = slice(hd * dv, (hd + 1) * dv)
        hs = hf_ref[:, sl] + hb_ref[:, sl]
        o = o_ref[:, sl]
        out_ref[:, sl] = (jax.nn.sigmoid(o) * _rms(hs, g_ref[:, sl])).astype(out_ref.dtype)


def mlstm_readout(h_f, h_b, o, g_norm, *, dv):
    r, width = h_f.shape
    row = pl.BlockSpec((ROW_TILE, width), lambda t: (t, 0))
    return pl.pallas_call(
        functools.partial(_readout_kernel, dv=dv),
        grid=(r // ROW_TILE,),
        in_specs=[row, row, row, pl.BlockSpec((1, width), lambda t: (0, 0))],
        out_specs=row,
        out_shape=jax.ShapeDtypeStruct((r, width), BF16),
        compiler_params=_params("arbitrary"),
        name="mlstm_readout",
    )(h_f, h_b, o, g_norm)


def _mla_prep_kernel(down_ref, krin_ref, qn_ref, kvn_ref, cos_ref, sin_ref, cq_ref, ckv_ref, kr_ref, *,
                     q_lora, kv_lora):
    cq_ref[...] = _rms(down_ref[:, :q_lora], qn_ref[...]).astype(cq_ref.dtype)
    ckv_ref[...] = _rms(down_ref[:, q_lora:q_lora + kv_lora], kvn_ref[...]).astype(ckv_ref.dtype)
    kr = krin_ref[...]
    lane = lax.broadcasted_iota(jnp.int32, kr.shape, 1)
    kr = jnp.where(lane < MLA_ROPE, kr, 0.0)
    kr_ref[...] = _rope_lanes(kr, cos_ref[...], sin_ref[...]).astype(kr_ref.dtype)


def mla_prep(down, kr_raw, q_norm, kv_norm, cos, sin, *, q_lora, kv_lora):
    r, width = down.shape
    return pl.pallas_call(
        functools.partial(_mla_prep_kernel, q_lora=q_lora, kv_lora=kv_lora),
        grid=(r // ROW_TILE,),
        in_specs=[pl.BlockSpec((ROW_TILE, width), lambda t: (t, 0)),
                  pl.BlockSpec((ROW_TILE, LANE), lambda t: (t, 0)),
                  pl.BlockSpec((1, q_lora), lambda t: (0, 0)),
                  pl.BlockSpec((1, kv_lora), lambda t: (0, 0)),
                  pl.BlockSpec((ROW_TILE, LANE), lambda t: (t, 0)),
                  pl.BlockSpec((ROW_TILE, LANE), lambda t: (t, 0))],
        out_specs=[pl.BlockSpec((ROW_TILE, q_lora), lambda t: (t, 0)),
                   pl.BlockSpec((ROW_TILE, kv_lora), lambda t: (t, 0)),
                   pl.BlockSpec((ROW_TILE, LANE), lambda t: (t, 0))],
        out_shape=[jax.ShapeDtypeStruct((r, q_lora), BF16),
                   jax.ShapeDtypeStruct((r, kv_lora), BF16),
                   jax.ShapeDtypeStruct((r, LANE), BF16)],
        compiler_params=_params("arbitrary"),
        name="mla_prep",
    )(down, kr_raw, q_norm, kv_norm, cos, sin)


def _softmax_attend(q, k_t, v):
    s = jnp.dot(q, k_t, preferred_element_type=F32)
    p = jnp.exp2(s - jnp.max(s, axis=1, keepdims=True))
    o = jnp.dot(p.astype(BF16), v, preferred_element_type=F32)
    return o / jnp.sum(p, axis=1, keepdims=True)


def _attn_lat_kernel(q_ref, knl_ref, krl_ref, vl_ref, knc_ref, krc_ref, vc_ref, o_ref, kcat_ref, vcat_ref):
    ctx_len = vc_ref.shape[0]

    @pl.when(pl.program_id(2) == 0)
    def _():
        kcat_ref[:MLA_NOPE, :ctx_len] = knc_ref[...]
        kcat_ref[MLA_NOPE:, :ctx_len] = krc_ref[...]
        kcat_ref[:MLA_NOPE, ctx_len:] = knl_ref[...]
        kcat_ref[MLA_NOPE:, ctx_len:] = krl_ref[...]
        vcat_ref[:ctx_len, :] = vc_ref[...]
        vcat_ref[ctx_len:, :] = vl_ref[...]

    k_t, v = kcat_ref[...], vcat_ref[...]
    sub = min(ATTN_SUB_TILE, q_ref.shape[0])
    for part in range(q_ref.shape[0] // sub):
        rows = slice(part * sub, (part + 1) * sub)
        o_ref[rows, :] = _softmax_attend(q_ref[rows, :], k_t, v).astype(o_ref.dtype)


def _attn_ctx_kernel(q_ref, kn_ref, kr_ref, v_ref, prev_ref, o_ref):
    del prev_ref
    k_t = jnp.concatenate([kn_ref[...], kr_ref[...]], axis=0)
    o_ref[...] = _softmax_attend(q_ref[...], k_t, v_ref[...]).astype(o_ref.dtype)


def mla_attention(q, k_t, kr_t, v, *, lay, heads, with_ctx):
    seq = lay.seq_tiles * ROW_TILE
    tq = min(ATTN_Q_TILE, seq)
    q_tiles = seq // tq
    out_rows = lay.rows if with_ctx else lay.lat_rows

    def keys_lat(row):
        return pl.BlockSpec((MLA_NOPE, seq), lambda b, h, j: (row(h), b))

    def keys_ctx(row):
        return pl.BlockSpec((MLA_NOPE, ROW_TILE), lambda b, h, j: (row(h), lay.ctx_tile(b)))

    own, shared = (lambda h: h), (lambda h: 0)
    att = pl.pallas_call(
        _attn_lat_kernel,
        grid=(lay.batch, heads, q_tiles),
        in_specs=[pl.BlockSpec((tq, MLA_QW), lambda b, h, j: (b * q_tiles + j, h)),
                  keys_lat(own), keys_lat(shared), pl.BlockSpec((seq, MLA_V), lambda b, h, j: (b, h)),
                  keys_ctx(own), keys_ctx(shared),
                  pl.BlockSpec((ROW_TILE, MLA_V), lambda b, h, j: (lay.ctx_tile(b), h))],
        out_specs=pl.BlockSpec((tq, MLA_V), lambda b, h, j: (b * q_tiles + j, h)),
        out_shape=jax.ShapeDtypeStruct((out_rows, heads * MLA_V), BF16),
        scratch_shapes=[pltpu.VMEM((MLA_QW, ROW_TILE + seq), BF16), pltpu.VMEM((ROW_TILE + seq, MLA_V), BF16)],
        compiler_params=_params("arbitrary", "arbitrary", "arbitrary"),
        name="mla_attention",
    )(q, k_t, kr_t, v, k_t, kr_t, v)
    if not with_ctx:
        return att

    def ctx_keys(row):
        return pl.BlockSpec((MLA_NOPE, ROW_TILE), lambda b, h: (row(h), lay.ctx_tile(b)))

    def ctx_rows(width):
        return pl.BlockSpec((ROW_TILE, width), lambda b, h: (lay.ctx_tile(b), h))

    return pl.pallas_call(
        _attn_ctx_kernel,
        grid=(lay.batch, heads),
        in_specs=[ctx_rows(MLA_QW), ctx_keys(own), ctx_keys(shared), ctx_rows(MLA_V),
                  pl.BlockSpec(memory_space=pl.ANY)],
        out_specs=ctx_rows(MLA_V),
        out_shape=jax.ShapeDtypeStruct(att.shape, att.dtype),
        input_output_aliases={4: 0},
        compiler_params=_params("arbitrary", "arbitrary"),
        name="mla_attention_ctx",
    )(q, k_t, kr_t, v, att)


def _conv_gate_kernel(ug_ref, uv_ref, pg_ref, pv_ref, ng_ref, nv_ref, wg_ref, wv_ref, bg_ref, bv_ref, o_ref, *, lay):
    t = pl.program_id(0)
    first, last = lay.seq_first(t), lay.seq_last(t)
    rows = ug_ref.shape[0]
    ridx = lax.broadcasted_iota(jnp.int32, ug_ref.shape, 0)

    def conv(u_ref, p_ref, n_ref, w_ref, b_ref):
        u = u_ref[...]
        prev_row = jnp.where(first, 0.0, p_ref[SUBLANE - 1:SUBLANE, :])
        next_row = jnp.where(last, 0.0, n_ref[0:1, :])
        up = jnp.where(ridx == 0, prev_row, pltpu.roll(u, 1, 0))
        un = jnp.where(ridx == rows - 1, next_row, pltpu.roll(u, rows - 1, 0))
        return up * w_ref[0:1, :] + u * w_ref[1:2, :] + un * w_ref[2:3, :] + b_ref[...]

    gate = conv(ug_ref, pg_ref, ng_ref, wg_ref, bg_ref)
    val = conv(uv_ref, pv_ref, nv_ref, wv_ref, bv_ref)
    o_ref[...] = (gate * jax.nn.sigmoid(gate) * val).astype(o_ref.dtype)


def conv_gate(u, conv_w, conv_b, *, lay, tn=1024):
    r, two_f = u.shape
    f = two_f // 2
    nb = f // tn
    per_tile = ROW_TILE // SUBLANE
    last_sub = r // SUBLANE - 1

    def main(off):
        return pl.BlockSpec((ROW_TILE, tn), lambda t, j: (t, off + j))

    def prev(off):
        return pl.BlockSpec((SUBLANE, tn), lambda t, j: (jnp.maximum(t * per_tile - 1, 0), off + j))

    def nxt(off):
        return pl.BlockSpec((SUBLANE, tn), lambda t, j: (jnp.minimum((t + 1) * per_tile, last_sub), off + j))

    def wspec(off):
        return pl.BlockSpec((3, tn), lambda t, j: (0, off + j))

    def bspec(off):
        return pl.BlockSpec((1, tn), lambda t, j: (0, off + j))

    return pl.pallas_call(
        functools.partial(_conv_gate_kernel, lay=lay),
        grid=(r // ROW_TILE, nb),
        in_specs=[main(0), main(nb), prev(0), prev(nb), nxt(0), nxt(nb), wspec(0), wspec(nb), bspec(0), bspec(nb)],
        out_specs=pl.BlockSpec((ROW_TILE, tn), lambda t, j: (t, j)),
        out_shape=jax.ShapeDtypeStruct((r, f), BF16),
        compiler_params=_params("arbitrary", "arbitrary"),
        name="conv_gate",
    )(u, u, u, u, u, u, conv_w, conv_w, conv_b.reshape(1, two_f), conv_b.reshape(1, two_f))


def _rope_tables(batch, ctx_len, seq):
    per_axis = MLA_ROPE // 2
    inv_freq = np.float32(ROPE_THETA) ** (-np.arange(0, per_axis, 2, dtype=np.float32) / np.float32(per_axis))
    pos = np.arange(seq)
    ang_r = (pos // GRID_W).astype(np.float32)[:, None] * inv_freq
    ang_c = (pos % GRID_W).astype(np.float32)[:, None] * inv_freq
    cos64 = np.concatenate([np.cos(ang_r)] * 2 + [np.cos(ang_c)] * 2, axis=1)
    sin64 = np.concatenate([-np.sin(ang_r), np.sin(ang_r), -np.sin(ang_c), np.sin(ang_c)], axis=1)
    cos_lat = np.tile(np.concatenate([cos64, cos64], axis=1), (batch, 1))
    sin_lat = np.tile(np.concatenate([sin64, sin64], axis=1), (batch, 1))
    cos = np.concatenate([cos_lat, np.ones((batch * ctx_len, LANE), np.float32)], axis=0)
    sin = np.concatenate([sin_lat, np.zeros((batch * ctx_len, LANE), np.float32)], axis=0)
    return jnp.asarray(cos, F32), jnp.asarray(sin, F32)


def kernel(x, c, ctx, c_ctx, mod_w, mod_b, norm_w, ml_w_in, ml_b_gate, ml_norm, ml_w_out, mla_w_down, mla_q_norm,
           mla_w_uq, mla_kv_norm, mla_w_ukv, mla_w_out, ffn_w_up, ffn_conv_w, ffn_conv_b, ffn_w_down):
    batch, seq, d = x.shape
    ctx_len = ctx.shape[1]
    depth = mod_w.shape[0]
    lay = Layout(batch, seq, ctx_len)
    ml_qk = d // 2
    dk, dv = ml_qk // ML_HEADS, d // ML_HEADS
    heads = d // MLA_NOPE
    q_lora, kv_lora = d // 4, d // 8

    h = (x.reshape(batch * seq, d), ctx.reshape(batch * ctx_len, d))
    cond = jnp.concatenate([c_ctx[None], c, jnp.zeros((MOD_ROWS - 1 - batch, d), F32)], axis=0)
    mods = adaln_all(cond, mod_w, mod_b).reshape(depth * MOD_ROWS * N_MOD, 1, d)
    gains = norm_w.reshape(depth * 4, 1, d)
    cos, sin = _rope_tables(batch, ctx_len, seq)
    norm = functools.partial(norm_modulate, mods=mods, norm_w=gains, lay=lay)
    w_in_t = jnp.swapaxes(ml_w_in, 1, 2)
    w_down_t = jnp.swapaxes(mla_w_down, 1, 2)

    f = None
    for i in range(depth):
        j = i // 2
        ctx_out = i < depth - 1
        if f is None:
            a = norm(h, layer=i, rows=lay.rows, pre_idx=0, shift_idx=0, scale_idx=1)[0]
        else:
            h, a = norm(h, layer=i, rows=lay.rows, y=f, y_layer=i - 1, gate_idx=5, post_idx=3,
                        pre_idx=0, shift_idx=0, scale_idx=1)
        if i % 2 == 0:
            qkv = matmul(a, w_in_t, layer=j, w_t=True, n_out=2 * ml_qk + d, out_dtype=BF16)
            o = matmul(a, w_in_t, layer=j, w_t=True, n_out=d, col0=2 * ml_qk + d)
            gates = matmul(a, w_in_t, layer=j, w_t=True, n_out=4 * ML_HEADS, col0=2 * ml_qk + 2 * d)
            bias = ml_b_gate[j].reshape(1, 4 * ML_HEADS)
            h_f, h_b = mlstm_scan(qkv, gates, bias, lay=lay, dk=dk, dv=dv)
            mixed = mlstm_readout(h_f, h_b, o, ml_norm[j].reshape(1, d), dv=dv)
        else:
            down = matmul(a, w_down_t, layer=j, w_t=True, n_out=q_lora + kv_lora)
            kr_raw = matmul(a, w_down_t, layer=j, w_t=True, n_out=LANE, col0=q_lora + kv_lora)
            cq, ckv, kr = mla_prep(down, kr_raw, mla_q_norm[j].reshape(1, q_lora),
                                   mla_kv_norm[j].reshape(1, kv_lora), cos, sin, q_lora=q_lora, kv_lora=kv_lora)
            q = matmul(cq, mla_w_uq, layer=j, pad_heads=True, n_out=heads * MLA_QW, out_dtype=BF16,
                       rope=(cos, sin), rope_scale=MLA_SCALE * math.log2(math.e))
            w_kv = mla_w_ukv[j].reshape(kv_lora, heads, MLA_NOPE + MLA_V)
            w_k_t = jnp.transpose(w_kv[:, :, :MLA_NOPE].reshape(kv_lora, heads * MLA_NOPE))
            w_v = w_kv[:, :, MLA_NOPE:].reshape(kv_lora, heads * MLA_V)
            k_t = matmul(w_k_t, ckv, w_t=True, out_dtype=BF16)
            v = matmul(ckv, w_v, out_dtype=BF16)
            mixed = mla_attention(q, k_t, jnp.transpose(kr), v, lay=lay, heads=heads, with_ctx=ctx_out)
        rows = mixed.shape[0]
        y = matmul(mixed, ml_w_out if i % 2 == 0 else mla_w_out, layer=j)
        h, a = norm(h, layer=i, rows=rows, y=y, gate_idx=2, post_idx=1, pre_idx=2, shift_idx=3, scale_idx=4)
        u = matmul(a, ffn_w_up, layer=i)
        g = conv_gate(u, ffn_conv_w[i], ffn_conv_b[i], lay=lay)
        f = matmul(g, ffn_w_down, layer=i)
    out = norm(h, layer=depth - 1, rows=lay.lat_rows, y=f, gate_idx=5, post_idx=3)[0]
    return out.reshape(batch, seq, d)
```

```python
import functools
import math

import jax
import jax.numpy as jnp
import numpy as np
from jax import lax
from jax.experimental import pallas as pl
from jax.experimental.pallas import tpu as pltpu

F32 = jnp.float32
BF16 = jnp.bfloat16

LANE = 128
SUBLANE = 8
VMEM_LIMIT_BYTES = 56 * 1024 * 1024
VMEM_BLOCK_FRACTION = 0.9

EPS = 1e-6
ML_HEADS = 8
GATE_CAP = 15.0
GRID_W = 64
ROPE_THETA = 10000.0
MLA_NOPE = 128
MLA_ROPE = 64
MLA_V = 128
MLA_SCALE = (MLA_NOPE + MLA_ROPE) ** -0.5
MLA_QW = 2 * LANE
N_MOD = 6
MOD_ROWS = 8

ROW_TILE = 256
ATTN_Q_TILE = 2048
ATTN_SUB_TILE = 256
SCAN_HEADS_PER_STEP = 2
CONV_COL_TILE = 2048


def _params(*sem):
    return pltpu.CompilerParams(dimension_semantics=sem, vmem_limit_bytes=VMEM_LIMIT_BYTES)


class Layout:
    def __init__(self, batch, seq, ctx_len):
        assert ctx_len == ROW_TILE and seq % (2 * ROW_TILE) == 0 and batch + 1 <= MOD_ROWS
        self.batch = batch
        self.seq_tiles = seq // ROW_TILE
        self.lat_tiles = batch * self.seq_tiles
        self.tiles = self.lat_tiles + batch
        self.lat_rows = batch * seq
        self.rows = self.tiles * ROW_TILE

    def mod_row(self, t):
        return jnp.where(t >= self.lat_tiles, 0, 1 + t // self.seq_tiles)

    def seq_first(self, t):
        return jnp.logical_or(t >= self.lat_tiles, t % self.seq_tiles == 0)

    def seq_last(self, t):
        return jnp.logical_or(t >= self.lat_tiles, t % self.seq_tiles == self.seq_tiles - 1)

    def ctx_tile(self, b):
        return self.lat_tiles + b


def _adaln_kernel(c_ref, w_ref, b_ref, o_ref):
    c = c_ref[...]
    s = c * jax.nn.sigmoid(c)
    o_ref[0] = jnp.dot(s, w_ref[0], preferred_element_type=F32) + b_ref[0]


def adaln_all(cond, mod_w, mod_b, tn=1024):
    depth, d, n = mod_w.shape
    return pl.pallas_call(
        _adaln_kernel,
        grid=(depth, n // tn),
        in_specs=[
            pl.BlockSpec((MOD_ROWS, d), lambda l, j: (0, 0)),
            pl.BlockSpec((1, d, tn), lambda l, j: (l, 0, j)),
            pl.BlockSpec((1, 1, tn), lambda l, j: (l, 0, j)),
        ],
        out_specs=pl.BlockSpec((1, MOD_ROWS, tn), lambda l, j: (l, 0, j)),
        out_shape=jax.ShapeDtypeStruct((depth, MOD_ROWS, n), F32),
        compiler_params=_params("arbitrary", "arbitrary"),
        name="adaln",
    )(cond, mod_w, mod_b.reshape(depth, 1, n))


def _rms(x, g):
    return x * lax.rsqrt(jnp.mean(x * x, axis=-1, keepdims=True) + EPS) * g


def _norm_kernel(*refs, has_y, has_a, split_tiles):
    it = iter(refs)
    if split_tiles is None:
        h = next(it)[...]
    else:
        lat_ref, ctx_ref = next(it), next(it)
        h = jnp.where(pl.program_id(0) < split_tiles, lat_ref[...], ctx_ref[...])
    if has_y:
        y_ref, gate_ref, gpost_ref = next(it), next(it), next(it)
    if has_a:
        gpre_ref, shift_ref, scale_ref = next(it), next(it), next(it)
    if has_y:
        hout_ref = next(it)
    if has_a:
        a_ref = next(it)
    if has_y:
        h = h + gate_ref[0] * _rms(y_ref[...], gpost_ref[0])
        hout_ref[...] = h
    if has_a:
        a = _rms(h, gpre_ref[0]) * (1.0 + scale_ref[0]) + shift_ref[0]
        a_ref[...] = a.astype(a_ref.dtype)


def norm_modulate(h, mods, norm_w, layer, *, lay, rows, y=None, y_layer=None, gate_idx=None, post_idx=None,
                  pre_idx=None, shift_idx=None, scale_idx=None):
    split = isinstance(h, tuple)
    d = h[0].shape[1] if split else h.shape[1]
    has_y, has_a = y is not None, pre_idx is not None
    y_layer = layer if y_layer is None else y_layer
    row_spec = pl.BlockSpec((ROW_TILE, d), lambda t: (t, 0))

    def mod_spec(lyr, which):
        return pl.BlockSpec((1, 1, d), lambda t: ((lyr * MOD_ROWS + lay.mod_row(t)) * N_MOD + which, 0, 0))

    def gain_spec(lyr, which):
        return pl.BlockSpec((1, 1, d), lambda t: (lyr * 4 + which, 0, 0))

    if split:
        args = list(h)
        in_specs = [pl.BlockSpec((ROW_TILE, d), lambda t: (jnp.minimum(t, lay.lat_tiles - 1), 0)),
                    pl.BlockSpec((ROW_TILE, d), lambda t: (jnp.maximum(t - lay.lat_tiles, 0), 0))]
    else:
        args, in_specs = [h], [row_spec]
    if has_y:
        args += [y, mods, norm_w]
        in_specs += [row_spec, mod_spec(y_layer, gate_idx), gain_spec(y_layer, post_idx)]
    if has_a:
        args += [norm_w, mods, mods]
        in_specs += [gain_spec(layer, pre_idx), mod_spec(layer, shift_idx), mod_spec(layer, scale_idx)]
    out_shape, out_specs = [], []
    if has_y:
        out_shape.append(jax.ShapeDtypeStruct((rows, d), F32))
        out_specs.append(row_spec)
    if has_a:
        out_shape.append(jax.ShapeDtypeStruct((rows, d), BF16))
        out_specs.append(row_spec)
    return pl.pallas_call(
        functools.partial(_norm_kernel, has_y=has_y, has_a=has_a, split_tiles=lay.lat_tiles if split else None),
        grid=(rows // ROW_TILE,),
        in_specs=in_specs,
        out_specs=out_specs,
        out_shape=out_shape,
        compiler_params=_params("arbitrary"),
        name="norm_modulate",
    )(*args)


def _rope_lanes(x, cos, sin):
    lane = lax.broadcasted_iota(jnp.int32, x.shape, 1)
    partner = jnp.where(lane % 32 < 16, pltpu.roll(x, LANE - 16, 1), pltpu.roll(x, 16, 1))
    return x * cos + partner * sin


def _pad_head_pairs(w):
    keep = lax.broadcasted_iota(jnp.int32, (w.shape[0], LANE), 1) < MLA_ROPE
    out = []
    for p in range(w.shape[1] // (3 * LANE)):
        nope0, mid, tail = (w[:, (3 * p + g) * LANE:(3 * p + g + 1) * LANE] for g in range(3))
        rolled = pltpu.roll(jnp.concatenate([mid, tail], axis=1), 2 * LANE - MLA_ROPE, 1)
        out += [nope0, jnp.where(keep, mid, 0.0), rolled[:, :LANE], jnp.where(keep, rolled[:, LANE:], 0.0)]
    return jnp.concatenate(out, axis=1)


def _mm_kernel(*refs, rope_scale, w_t, pad_heads):
    if rope_scale is not None:
        a_ref, w_ref, cos_ref, sin_ref, o_ref = refs
    else:
        a_ref, w_ref, o_ref = refs
    w = _pad_head_pairs(w_ref[...]) if pad_heads else w_ref[...]
    a, w = a_ref[...].astype(BF16), w.astype(BF16)
    if w_t:
        acc = lax.dot_general(a, w, (((1,), (1,)), ((), ())), preferred_element_type=F32)
    else:
        acc = jnp.dot(a, w, preferred_element_type=F32)
    if rope_scale is not None:
        cos, sin = cos_ref[...], sin_ref[...]
        for g in range(acc.shape[1] // LANE):
            blk = acc[:, g * LANE:(g + 1) * LANE]
            if g % 2 == 1:
                blk = _rope_lanes(blk, cos, sin)
            o_ref[:, g * LANE:(g + 1) * LANE] = (blk * rope_scale).astype(o_ref.dtype)
    else:
        o_ref[...] = acc.astype(o_ref.dtype)


def _pick_blocks(r, k, n_out, a_itemsize, out_itemsize, tm_cap):
    tm = max(t for t in range(16, tm_cap + 1, 16) if r % t == 0)
    budget = VMEM_BLOCK_FRACTION * VMEM_LIMIT_BYTES
    candidates = [t for t in (2048, 1024, 512, 256) if n_out % t == 0] or [n_out]
    for tn in candidates:
        fixed = k * tn * (2 * 4 + 2) + 2 * tm * tn * out_itemsize
        for a_copies in (2, 1):
            if fixed + a_copies * tm * k * a_itemsize <= budget:
                return tm, tn, a_copies == 1
    raise ValueError("no matmul tiling fits VMEM")


def matmul(a, w, *, layer=None, n_out=None, w_t=False, col0=0, tm_cap=1088, out_dtype=F32, rope=None,
           rope_scale=None, pad_heads=False):
    r, k = a.shape
    n_w, k_w = (w.shape[-2], w.shape[-1]) if w_t else (w.shape[-1], w.shape[-2])
    n_out = n_w if n_out is None else n_out
    tm, tn, single_a = _pick_blocks(r, k, n_out, a.dtype.itemsize, jnp.dtype(out_dtype).itemsize, tm_cap)
    assert n_out % tn == 0 and col0 % tn == 0 and k_w == k
    cb0 = col0 // tn
    if pad_heads:
        assert not w_t and tn % (4 * LANE) == 0 and n_w * 4 == n_out * 3
    w_block = (tn, k) if w_t else (k, tn * 3 // 4 if pad_heads else tn)
    w_index = (lambda j: (cb0 + j, 0)) if w_t else (lambda j: (0, cb0 + j))
    if layer is None:
        w_spec = pl.BlockSpec(w_block, lambda i, j: w_index(j))
    else:
        w_spec = pl.BlockSpec((None,) + w_block, lambda i, j: (layer,) + w_index(j))
    args = [a, w]
    a_mode = {"pipeline_mode": pl.Buffered(1)} if single_a else {}
    in_specs = [pl.BlockSpec((tm, k), lambda i, j: (i, 0), **a_mode), w_spec]
    if rope is not None:
        args += list(rope)
        in_specs += [pl.BlockSpec((tm, LANE), lambda i, j: (i, 0))] * 2
    return pl.pallas_call(
        functools.partial(_mm_kernel, rope_scale=rope_scale if rope is not None else None, w_t=w_t,
                          pad_heads=pad_heads),
        grid=(r // tm, n_out // tn),
        in_specs=in_specs,
        out_specs=pl.BlockSpec((tm, tn), lambda i, j: (i, j)),
        out_shape=jax.ShapeDtypeStruct((r, n_out), out_dtype),
        compiler_params=_params("arbitrary", "arbitrary"),
        name="matmul",
    )(*args)


def _col_to_row(col, eye):
    return jnp.sum(jnp.where(eye, col, 0.0), axis=0, keepdims=True)


def _mlstm_chunk(q, k, v, ig, lf, mask, eye, c_ref, n_ref, m_ref):
    neg_inf = -jnp.inf
    lf_row = _col_to_row(lf, eye)
    b = jnp.sum(jnp.where(mask, lf_row, 0.0), axis=1, keepdims=True)
    b_end = jnp.sum(lf, axis=0, keepdims=True)
    src = ig - b
    dmat = jnp.where(mask, b + _col_to_row(src, eye), neg_inf)
    m_prev = m_ref[...]
    inter = b + m_prev
    m_out = jnp.maximum(inter, jnp.max(dmat, axis=1, keepdims=True))
    w_intra = jnp.exp(dmat - m_out)
    w_inter = jnp.exp(inter - m_out)
    qk = lax.dot_general(q, k, (((1,), (1,)), ((), ())), preferred_element_type=F32)
    s = qk * w_intra
    c_prev = c_ref[...]
    num = jnp.dot(s.astype(BF16), v, preferred_element_type=F32)
    num = num + w_inter * jnp.dot(q, c_prev.astype(BF16), preferred_element_type=F32)
    qn = jnp.sum(q.astype(F32) * n_ref[...], axis=1, keepdims=True)
    den = jnp.sum(s, axis=1, keepdims=True) + w_inter * qn
    den = jnp.maximum(jnp.abs(den), jnp.exp(-m_out))
    h = num / den
    g = b_end + src
    m_new = jnp.maximum(b_end + m_prev, jnp.max(g, axis=0, keepdims=True))
    wg = jnp.exp(g - m_new)
    decay = jnp.exp(b_end + m_prev - m_new)
    kw = k.astype(F32) * wg
    c_ref[...] = decay * c_prev + lax.dot_general(
        kw.astype(BF16), v, (((0,), (0,)), ((), ())), preferred_element_type=F32)
    n_ref[...] = decay * n_ref[...] + jnp.sum(kw, axis=0, keepdims=True)
    m_ref[...] = m_new
    return h


def _mlstm_kernel(qf_ref, kf_ref, vf_ref, gf_ref, qb_ref, kb_ref, vb_ref, gb_ref, bias_ref,
                  hf_ref, hb_ref, cf_ref, nf_ref, mf_ref, cb_ref, nb_ref, mb_ref, *, dk_scale):
    heads_per_step = cf_ref.shape[0]
    dk, dv = cf_ref.shape[1], cf_ref.shape[2]

    @pl.when(pl.program_id(2) == 0)
    def _():
        for ref in (cf_ref, nf_ref, mf_ref, cb_ref, nb_ref, mb_ref):
            ref[...] = jnp.zeros_like(ref)

    chunk = qf_ref.shape[0]
    row = lax.broadcasted_iota(jnp.int32, (chunk, chunk), 0)
    col = lax.broadcasted_iota(jnp.int32, (chunk, chunk), 1)
    eye = row == col
    lane = lax.broadcasted_iota(jnp.int32, gf_ref.shape, 1)

    def gate_cols(g_ref, first, head):
        g = g_ref[...] + bias_ref[...]
        g = GATE_CAP * jnp.tanh(g / GATE_CAP)
        ig = jnp.sum(jnp.where(lane == first * ML_HEADS + head, g, 0.0), axis=1, keepdims=True)
        fg = jnp.sum(jnp.where(lane == (first + 1) * ML_HEADS + head, g, 0.0), axis=1, keepdims=True)
        return ig, jax.nn.log_sigmoid(fg)

    def run(i, q_ref, k_ref, v_ref, g_ref, first, mask, c_ref, n_ref, m_ref, h_ref):
        ig, lf = gate_cols(g_ref, first, pl.program_id(1) * heads_per_step + i)
        qk_cols, v_cols = slice(i * dk, (i + 1) * dk), slice(i * dv, (i + 1) * dv)
        q = q_ref[:, qk_cols] * dk_scale
        h_ref[:, v_cols] = _mlstm_chunk(q, k_ref[:, qk_cols], v_ref[:, v_cols], ig, lf, mask, eye,
                                        c_ref.at[i], n_ref.at[i], m_ref.at[i])

    for i in range(heads_per_step):
        run(i, qf_ref, kf_ref, vf_ref, gf_ref, 0, col <= row, cf_ref, nf_ref, mf_ref, hf_ref)
        run(i, qb_ref, kb_ref, vb_ref, gb_ref, 2, col >= row, cb_ref, nb_ref, mb_ref, hb_ref)


def mlstm_scan(qkv, gates, gate_bias, *, lay, dk, dv):
    r = qkv.shape[0]
    n_gates = gates.shape[1]
    st = lay.seq_tiles

    def fwd_tile(b, c):
        return jnp.where(c == 0, lay.ctx_tile(b), b * st + c - 1)

    def bwd_tile(b, c):
        return jnp.where(c == 0, lay.ctx_tile(b), b * st + st - c)

    hp = SCAN_HEADS_PER_STEP
    assert ML_HEADS % hp == 0

    def specs(tile):
        return [
            pl.BlockSpec((ROW_TILE, hp * dk), lambda b, h, c: (tile(b, c), h)),
            pl.BlockSpec((ROW_TILE, hp * dk), lambda b, h, c: (tile(b, c), ML_HEADS // hp + h)),
            pl.BlockSpec((ROW_TILE, hp * dv), lambda b, h, c: (tile(b, c), (2 * ML_HEADS * dk) // (hp * dv) + h)),
            pl.BlockSpec((ROW_TILE, n_gates), lambda b, h, c: (tile(b, c), 0)),
        ]

    def out_spec(tile):
        return pl.BlockSpec((ROW_TILE, hp * dv), lambda b, h, c: (tile(b, c), h))

    state = [pltpu.VMEM((hp, dk, dv), F32), pltpu.VMEM((hp, 1, dk), F32), pltpu.VMEM((hp, 1, 1), F32)]
    return pl.pallas_call(
        functools.partial(_mlstm_kernel, dk_scale=dk ** -0.5),
        grid=(lay.batch, ML_HEADS // hp, st + 1),
        in_specs=specs(fwd_tile) + specs(bwd_tile) + [pl.BlockSpec((1, n_gates), lambda b, h, c: (0, 0))],
        out_specs=[out_spec(fwd_tile), out_spec(bwd_tile)],
        out_shape=[jax.ShapeDtypeStruct((r, ML_HEADS * dv), F32)] * 2,
        scratch_shapes=state + state,
        compiler_params=_params("arbitrary", "arbitrary", "arbitrary"),
        name="mlstm_scan",
    )(qkv, qkv, qkv, gates, qkv, qkv, qkv, gates, gate_bias)


def _readout_kernel(hf_ref, hb_ref, o_ref, g_ref, out_ref, *, dv):
    for hd in range(ML_HEADS):
        sl = slice(hd * dv, (hd + 1) * dv)
        hs = hf_ref[:, sl] + hb_ref[:, sl]
        o = o_ref[:, sl]
        out_ref[:, sl] = (jax.nn.sigmoid(o) * _rms(hs, g_ref[:, sl])).astype(out_ref.dtype)


def mlstm_readout(h_f, h_b, o, g_norm, *, dv):
    r, width = h_f.shape
    row = pl.BlockSpec((ROW_TILE, width), lambda t: (t, 0))
    return pl.pallas_call(
        functools.partial(_readout_kernel, dv=dv),
        grid=(r // ROW_TILE,),
        in_specs=[row, row, row, pl.BlockSpec((1, width), lambda t: (0, 0))],
        out_specs=row,
        out_shape=jax.ShapeDtypeStruct((r, width), BF16),
        compiler_params=_params("arbitrary"),
        name="mlstm_readout",
    )(h_f, h_b, o, g_norm)


def _mla_prep_kernel(down_ref, krin_ref, qn_ref, kvn_ref, cos_ref, sin_ref, cq_ref, ckv_ref, kr_ref, *,
                     q_lora, kv_lora):
    cq_ref[...] = _rms(down_ref[:, :q_lora], qn_ref[...]).astype(cq_ref.dtype)
    ckv_ref[...] = _rms(down_ref[:, q_lora:q_lora + kv_lora], kvn_ref[...]).astype(ckv_ref.dtype)
    kr = krin_ref[...]
    lane = lax.broadcasted_iota(jnp.int32, kr.shape, 1)
    kr = jnp.where(lane < MLA_ROPE, kr, 0.0)
    kr_ref[...] = _rope_lanes(kr, cos_ref[...], sin_ref[...]).astype(kr_ref.dtype)


def mla_prep(down, kr_raw, q_norm, kv_norm, cos, sin, *, q_lora, kv_lora):
    r, width = down.shape
    return pl.pallas_call(
        functools.partial(_mla_prep_kernel, q_lora=q_lora, kv_lora=kv_lora),
        grid=(r // ROW_TILE,),
        in_specs=[pl.BlockSpec((ROW_TILE, width), lambda t: (t, 0)),
                  pl.BlockSpec((ROW_TILE, LANE), lambda t: (t, 0)),
                  pl.BlockSpec((1, q_lora), lambda t: (0, 0)),
                  pl.BlockSpec((1, kv_lora), lambda t: (0, 0)),
                  pl.BlockSpec((ROW_TILE, LANE), lambda t: (t, 0)),
                  pl.BlockSpec((ROW_TILE, LANE), lambda t: (t, 0))],
        out_specs=[pl.BlockSpec((ROW_TILE, q_lora), lambda t: (t, 0)),
                   pl.BlockSpec((ROW_TILE, kv_lora), lambda t: (t, 0)),
                   pl.BlockSpec((ROW_TILE, LANE), lambda t: (t, 0))],
        out_shape=[jax.ShapeDtypeStruct((r, q_lora), BF16),
                   jax.ShapeDtypeStruct((r, kv_lora), BF16),
                   jax.ShapeDtypeStruct((r, LANE), BF16)],
        compiler_params=_params("arbitrary"),
        name="mla_prep",
    )(down, kr_raw, q_norm, kv_norm, cos, sin)


def _softmax_attend(q, k_t, v):
    s = jnp.dot(q, k_t, preferred_element_type=F32)
    p = jnp.exp2(s - jnp.max(s, axis=1, keepdims=True))
    o = jnp.dot(p.astype(BF16), v, preferred_element_type=F32)
    return o / jnp.sum(p, axis=1, keepdims=True)


def _attn_lat_kernel(q_ref, knl_ref, krl_ref, vl_ref, knc_ref, krc_ref, vc_ref, o_ref, kcat_ref, vcat_ref):
    ctx_len = vc_ref.shape[0]

    @pl.when(pl.program_id(2) == 0)
    def _():
        kcat_ref[:MLA_NOPE, :ctx_len] = knc_ref[...]
        kcat_ref[MLA_NOPE:, :ctx_len] = krc_ref[...]
        kcat_ref[:MLA_NOPE, ctx_len:] = knl_ref[...]
        kcat_ref[MLA_NOPE:, ctx_len:] = krl_ref[...]
        vcat_ref[:ctx_len, :] = vc_ref[...]
        vcat_ref[ctx_len:, :] = vl_ref[...]

    k_t, v = kcat_ref[...], vcat_ref[...]
    sub = min(ATTN_SUB_TILE, q_ref.shape[0])
    for part in range(q_ref.shape[0] // sub):
        rows = slice(part * sub, (part + 1) * sub)
        o_ref[rows, :] = _softmax_attend(q_ref[rows, :], k_t, v).astype(o_ref.dtype)


def _attn_ctx_kernel(q_ref, kn_ref, kr_ref, v_ref, prev_ref, o_ref):
    del prev_ref
    k_t = jnp.concatenate([kn_ref[...], kr_ref[...]], axis=0)
    o_ref[...] = _softmax_attend(q_ref[...], k_t, v_ref[...]).astype(o_ref.dtype)


def mla_attention(q, k_t, kr_t, v, *, lay, heads, with_ctx):
    seq = lay.seq_tiles * ROW_TILE
    tq = min(ATTN_Q_TILE, seq)
    q_tiles = seq // tq
    out_rows = lay.rows if with_ctx else lay.lat_rows

    def keys_lat(row):
        return pl.BlockSpec((MLA_NOPE, seq), lambda b, h, j: (row(h), b))

    def keys_ctx(row):
        return pl.BlockSpec((MLA_NOPE, ROW_TILE), lambda b, h, j: (row(h), lay.ctx_tile(b)))

    own, shared = (lambda h: h), (lambda h: 0)
    att = pl.pallas_call(
        _attn_lat_kernel,
        grid=(lay.batch, heads, q_tiles),
        in_specs=[pl.BlockSpec((tq, MLA_QW), lambda b, h, j: (b * q_tiles + j, h)),
                  keys_lat(own), keys_lat(shared), pl.BlockSpec((seq, MLA_V), lambda b, h, j: (b, h)),
                  keys_ctx(own), keys_ctx(shared),
                  pl.BlockSpec((ROW_TILE, MLA_V), lambda b, h, j: (lay.ctx_tile(b), h))],
        out_specs=pl.BlockSpec((tq, MLA_V), lambda b, h, j: (b * q_tiles + j, h)),
        out_shape=jax.ShapeDtypeStruct((out_rows, heads * MLA_V), BF16),
        scratch_shapes=[pltpu.VMEM((MLA_QW, ROW_TILE + seq), BF16), pltpu.VMEM((ROW_TILE + seq, MLA_V), BF16)],
        compiler_params=_params("arbitrary", "arbitrary", "arbitrary"),
        name="mla_attention",
    )(q, k_t, kr_t, v, k_t, kr_t, v)
    if not with_ctx:
        return att

    def ctx_keys(row):
        return pl.BlockSpec((MLA_NOPE, ROW_TILE), lambda b, h: (row(h), lay.ctx_tile(b)))

    def ctx_rows(width):
        return pl.BlockSpec((ROW_TILE, width), lambda b, h: (lay.ctx_tile(b), h))

    return pl.pallas_call(
        _attn_ctx_kernel,
        grid=(lay.batch, heads),
        in_specs=[ctx_rows(MLA_QW), ctx_keys(own), ctx_keys(shared), ctx_rows(MLA_V),
                  pl.BlockSpec(memory_space=pl.ANY)],
        out_specs=ctx_rows(MLA_V),
        out_shape=jax.ShapeDtypeStruct(att.shape, att.dtype),
        input_output_aliases={4: 0},
        compiler_params=_params("arbitrary", "arbitrary"),
        name="mla_attention_ctx",
    )(q, k_t, kr_t, v, att)


def _conv_gate_kernel(ug_ref, uv_ref, pg_ref, pv_ref, ng_ref, nv_ref, wg_ref, wv_ref, bg_ref, bv_ref, o_ref, *, lay):
    t = pl.program_id(0)
    first, last = lay.seq_first(t), lay.seq_last(t)
    rows = ug_ref.shape[0]
    ridx = lax.broadcasted_iota(jnp.int32, ug_ref.shape, 0)

    def conv(u_ref, p_ref, n_ref, w_ref, b_ref):
        u = u_ref[...]
        prev_row = jnp.where(first, 0.0, p_ref[SUBLANE - 1:SUBLANE, :])
        next_row = jnp.where(last, 0.0, n_ref[0:1, :])
        up = jnp.where(ridx == 0, prev_row, pltpu.roll(u, 1, 0))
        un = jnp.where(ridx == rows - 1, next_row, pltpu.roll(u, rows - 1, 0))
        return up * w_ref[0:1, :] + u * w_ref[1:2, :] + un * w_ref[2:3, :] + b_ref[...]

    gate = conv(ug_ref, pg_ref, ng_ref, wg_ref, bg_ref)
    val = conv(uv_ref, pv_ref, nv_ref, wv_ref, bv_ref)
    o_ref[...] = (gate * jax.nn.sigmoid(gate) * val).astype(o_ref.dtype)


def conv_gate(u, conv_w, conv_b, *, lay):
    r, two_f = u.shape
    f = two_f // 2
    tn = next(t for t in (CONV_COL_TILE, CONV_COL_TILE // 2, CONV_COL_TILE // 4, LANE) if f % t == 0)
    nb = f // tn
    per_tile = ROW_TILE // SUBLANE
    last_sub = r // SUBLANE - 1

    def main(off):
        return pl.BlockSpec((ROW_TILE, tn), lambda t, j: (t, off + j))

    def prev(off):
        return pl.BlockSpec((SUBLANE, tn), lambda t, j: (jnp.maximum(t * per_tile - 1, 0), off + j))

    def nxt(off):
        return pl.BlockSpec((SUBLANE, tn), lambda t, j: (jnp.minimum((t + 1) * per_tile, last_sub), off + j))

    def wspec(off):
        return pl.BlockSpec((3, tn), lambda t, j: (0, off + j))

    def bspec(off):
        return pl.BlockSpec((1, tn), lambda t, j: (0, off + j))

    return pl.pallas_call(
        functools.partial(_conv_gate_kernel, lay=lay),
        grid=(r // ROW_TILE, nb),
        in_specs=[main(0), main(nb), prev(0), prev(nb), nxt(0), nxt(nb), wspec(0), wspec(nb), bspec(0), bspec(nb)],
        out_specs=pl.BlockSpec((ROW_TILE, tn), lambda t, j: (t, j)),
        out_shape=jax.ShapeDtypeStruct((r, f), BF16),
        compiler_params=_params("arbitrary", "arbitrary"),
        name="conv_gate",
    )(u, u, u, u, u, u, conv_w, conv_w, conv_b.reshape(1, two_f), conv_b.reshape(1, two_f))


def _rope_tables(batch, ctx_len, seq):
    per_axis = MLA_ROPE // 2
    inv_freq = np.float32(ROPE_THETA) ** (-np.arange(0, per_axis, 2, dtype=np.float32) / np.float32(per_axis))
    pos = np.arange(seq)
    ang_r = (pos // GRID_W).astype(np.float32)[:, None] * inv_freq
    ang_c = (pos % GRID_W).astype(np.float32)[:, None] * inv_freq
    cos64 = np.concatenate([np.cos(ang_r)] * 2 + [np.cos(ang_c)] * 2, axis=1)
    sin64 = np.concatenate([-np.sin(ang_r), np.sin(ang_r), -np.sin(ang_c), np.sin(ang_c)], axis=1)
    cos_lat = np.tile(np.concatenate([cos64, cos64], axis=1), (batch, 1))
    sin_lat = np.tile(np.concatenate([sin64, sin64], axis=1), (batch, 1))
    cos = np.concatenate([cos_lat, np.ones((batch * ctx_len, LANE), np.float32)], axis=0)
    sin = np.concatenate([sin_lat, np.zeros((batch * ctx_len, LANE), np.float32)], axis=0)
    return jnp.asarray(cos, F32), jnp.asarray(sin, F32)


def kernel(x, c, ctx, c_ctx, mod_w, mod_b, norm_w, ml_w_in, ml_b_gate, ml_norm, ml_w_out, mla_w_down, mla_q_norm,
           mla_w_uq, mla_kv_norm, mla_w_ukv, mla_w_out, ffn_w_up, ffn_conv_w, ffn_conv_b, ffn_w_down):
    batch, seq, d = x.shape
    ctx_len = ctx.shape[1]
    depth = mod_w.shape[0]
    lay = Layout(batch, seq, ctx_len)
    ml_qk = d // 2
    dk, dv = ml_qk // ML_HEADS, d // ML_HEADS
    heads = d // MLA_NOPE
    q_lora, kv_lora = d // 4, d // 8

    h = (x.reshape(batch * seq, d), ctx.reshape(batch * ctx_len, d))
    cond = jnp.concatenate([c_ctx[None], c, jnp.zeros((MOD_ROWS - 1 - batch, d), F32)], axis=0)
    mods = adaln_all(cond, mod_w, mod_b).reshape(depth * MOD_ROWS * N_MOD, 1, d)
    gains = norm_w.reshape(depth * 4, 1, d)
    cos, sin = _rope_tables(batch, ctx_len, seq)
    norm = functools.partial(norm_modulate, mods=mods, norm_w=gains, lay=lay)
    w_in_t = jnp.swapaxes(ml_w_in, 1, 2)
    w_down_t = jnp.swapaxes(mla_w_down, 1, 2)

    f = None
    for i in range(depth):
        j = i // 2
        ctx_out = i < depth - 1
        if f is None:
            a = norm(h, layer=i, rows=lay.rows, pre_idx=0, shift_idx=0, scale_idx=1)[0]
        else:
            h, a = norm(h, layer=i, rows=lay.rows, y=f, y_layer=i - 1, gate_idx=5, post_idx=3,
                        pre_idx=0, shift_idx=0, scale_idx=1)
        if i % 2 == 0:
            qkv = matmul(a, w_in_t, layer=j, w_t=True, n_out=2 * ml_qk + d, out_dtype=BF16)
            o = matmul(a, w_in_t, layer=j, w_t=True, n_out=d, col0=2 * ml_qk + d)
            gates = matmul(a, w_in_t, layer=j, w_t=True, n_out=4 * ML_HEADS, col0=2 * ml_qk + 2 * d)
            bias = ml_b_gate[j].reshape(1, 4 * ML_HEADS)
            h_f, h_b = mlstm_scan(qkv, gates, bias, lay=lay, dk=dk, dv=dv)
            mixed = mlstm_readout(h_f, h_b, o, ml_norm[j].reshape(1, d), dv=dv)
        else:
            down = matmul(a, w_down_t, layer=j, w_t=True, n_out=q_lora + kv_lora)
            kr_raw = matmul(a, w_down_t, layer=j, w_t=True, n_out=LANE, col0=q_lora + kv_lora)
            cq, ckv, kr = mla_prep(down, kr_raw, mla_q_norm[j].reshape(1, q_lora),
                                   mla_kv_norm[j].reshape(1, kv_lora), cos, sin, q_lora=q_lora, kv_lora=kv_lora)
            q = matmul(cq, mla_w_uq, layer=j, pad_heads=True, n_out=heads * MLA_QW, out_dtype=BF16,
                       rope=(cos, sin), rope_scale=MLA_SCALE * math.log2(math.e))
            w_kv = mla_w_ukv[j].reshape(kv_lora, heads, MLA_NOPE + MLA_V)
            w_k_t = jnp.transpose(w_kv[:, :, :MLA_NOPE].reshape(kv_lora, heads * MLA_NOPE))
            w_v = w_kv[:, :, MLA_NOPE:].reshape(kv_lora, heads * MLA_V)
            k_t = matmul(w_k_t, ckv, w_t=True, out_dtype=BF16)
            v = matmul(ckv, w_v, out_dtype=BF16)
            mixed = mla_attention(q, k_t, jnp.transpose(kr), v, lay=lay, heads=heads, with_ctx=ctx_out)
        rows = mixed.shape[0]
        y = matmul(mixed, ml_w_out if i % 2 == 0 else mla_w_out, layer=j)
        h, a = norm(h, layer=i, rows=rows, y=y, gate_idx=2, post_idx=1, pre_idx=2, shift_idx=3, scale_idx=4)
        u = matmul(a, ffn_w_up, layer=i)
        g = conv_gate(u, ffn_conv_w[i], ffn_conv_b[i], lay=lay)
        f = matmul(g, ffn_w_down, layer=i)
    out = norm(h, layer=depth - 1, rows=lay.lat_rows, y=f, gate_idx=5, post_idx=3)[0]
    return out.reshape(batch, seq, d)
```

```python
import functools
import math

import jax
import jax.numpy as jnp
import numpy as np
from jax import lax
from jax.experimental import pallas as pl
from jax.experimental.pallas import tpu as pltpu

F32 = jnp.float32
BF16 = jnp.bfloat16

LANE = 128
SUBLANE = 8
VMEM_LIMIT_BYTES = 56 * 1024 * 1024
VMEM_BLOCK_FRACTION = 0.9

EPS = 1e-6
ML_HEADS = 8
GATE_CAP = 15.0
GRID_W = 64
ROPE_THETA = 10000.0
MLA_NOPE = 128
MLA_ROPE = 64
MLA_V = 128
MLA_SCALE = (MLA_NOPE + MLA_ROPE) ** -0.5
MLA_QW = 2 * LANE
N_MOD = 6
MOD_ROWS = 8

ROW_TILE = 256
ATTN_Q_TILE = 2048
ATTN_SUB_TILE = 256
SCAN_HEADS_PER_STEP = 2
CONV_COL_TILE = 3072


def _params(*sem):
    return pltpu.CompilerParams(dimension_semantics=sem, vmem_limit_bytes=VMEM_LIMIT_BYTES)


class Layout:
    def __init__(self, batch, seq, ctx_len):
        assert ctx_len == ROW_TILE and seq % (2 * ROW_TILE) == 0 and batch + 1 <= MOD_ROWS
        self.batch = batch
        self.seq_tiles = seq // ROW_TILE
        self.lat_tiles = batch * self.seq_tiles
        self.tiles = self.lat_tiles + batch
        self.lat_rows = batch * seq
        self.rows = self.tiles * ROW_TILE

    def mod_row(self, t):
        return jnp.where(t >= self.lat_tiles, 0, 1 + t // self.seq_tiles)

    def seq_first(self, t):
        return jnp.logical_or(t >= self.lat_tiles, t % self.seq_tiles == 0)

    def seq_last(self, t):
        return jnp.logical_or(t >= self.lat_tiles, t % self.seq_tiles == self.seq_tiles - 1)

    def ctx_tile(self, b):
        return self.lat_tiles + b


def _adaln_kernel(c_ref, w_ref, b_ref, o_ref):
    c = c_ref[...]
    s = c * jax.nn.sigmoid(c)
    o_ref[0] = jnp.dot(s, w_ref[0], preferred_element_type=F32) + b_ref[0]


def adaln_all(cond, mod_w, mod_b, tn=1024):
    depth, d, n = mod_w.shape
    return pl.pallas_call(
        _adaln_kernel,
        grid=(depth, n // tn),
        in_specs=[
            pl.BlockSpec((MOD_ROWS, d), lambda l, j: (0, 0)),
            pl.BlockSpec((1, d, tn), lambda l, j: (l, 0, j)),
            pl.BlockSpec((1, 1, tn), lambda l, j: (l, 0, j)),
        ],
        out_specs=pl.BlockSpec((1, MOD_ROWS, tn), lambda l, j: (l, 0, j)),
        out_shape=jax.ShapeDtypeStruct((depth, MOD_ROWS, n), F32),
        compiler_params=_params("arbitrary", "arbitrary"),
        name="adaln",
    )(cond, mod_w, mod_b.reshape(depth, 1, n))


def _rms(x, g):
    return x * lax.rsqrt(jnp.mean(x * x, axis=-1, keepdims=True) + EPS) * g


def _norm_kernel(*refs, has_y, has_a, split_tiles):
    it = iter(refs)
    if split_tiles is None:
        h = next(it)[...]
    else:
        lat_ref, ctx_ref = next(it), next(it)
        h = jnp.where(pl.program_id(0) < split_tiles, lat_ref[...], ctx_ref[...])
    if has_y:
        y_ref, gate_ref, gpost_ref = next(it), next(it), next(it)
    if has_a:
        gpre_ref, shift_ref, scale_ref = next(it), next(it), next(it)
    if has_y:
        hout_ref = next(it)
    if has_a:
        a_ref = next(it)
    if has_y:
        h = h + gate_ref[0] * _rms(y_ref[...], gpost_ref[0])
        hout_ref[...] = h
    if has_a:
        a = _rms(h, gpre_ref[0]) * (1.0 + scale_ref[0]) + shift_ref[0]
        a_ref[...] = a.astype(a_ref.dtype)


def norm_modulate(h, mods, norm_w, layer, *, lay, rows, y=None, y_layer=None, gate_idx=None, post_idx=None,
                  pre_idx=None, shift_idx=None, scale_idx=None):
    split = isinstance(h, tuple)
    d = h[0].shape[1] if split else h.shape[1]
    has_y, has_a = y is not None, pre_idx is not None
    y_layer = layer if y_layer is None else y_layer
    row_spec = pl.BlockSpec((ROW_TILE, d), lambda t: (t, 0))

    def mod_spec(lyr, which):
        return pl.BlockSpec((1, 1, d), lambda t: ((lyr * MOD_ROWS + lay.mod_row(t)) * N_MOD + which, 0, 0))

    def gain_spec(lyr, which):
        return pl.BlockSpec((1, 1, d), lambda t: (lyr * 4 + which, 0, 0))

    if split:
        args = list(h)
        in_specs = [pl.BlockSpec((ROW_TILE, d), lambda t: (jnp.minimum(t, lay.lat_tiles - 1), 0)),
                    pl.BlockSpec((ROW_TILE, d), lambda t: (jnp.maximum(t - lay.lat_tiles, 0), 0))]
    else:
        args, in_specs = [h], [row_spec]
    if has_y:
        args += [y, mods, norm_w]
        in_specs += [row_spec, mod_spec(y_layer, gate_idx), gain_spec(y_layer, post_idx)]
    if has_a:
        args += [norm_w, mods, mods]
        in_specs += [gain_spec(layer, pre_idx), mod_spec(layer, shift_idx), mod_spec(layer, scale_idx)]
    out_shape, out_specs = [], []
    if has_y:
        out_shape.append(jax.ShapeDtypeStruct((rows, d), F32))
        out_specs.append(row_spec)
    if has_a:
        out_shape.append(jax.ShapeDtypeStruct((rows, d), BF16))
        out_specs.append(row_spec)
    return pl.pallas_call(
        functools.partial(_norm_kernel, has_y=has_y, has_a=has_a, split_tiles=lay.lat_tiles if split else None),
        grid=(rows // ROW_TILE,),
        in_specs=in_specs,
        out_specs=out_specs,
        out_shape=out_shape,
        compiler_params=_params("arbitrary"),
        name="norm_modulate",
    )(*args)


def _rope_lanes(x, cos, sin):
    lane = lax.broadcasted_iota(jnp.int32, x.shape, 1)
    partner = jnp.where(lane % 32 < 16, pltpu.roll(x, LANE - 16, 1), pltpu.roll(x, 16, 1))
    return x * cos + partner * sin


def _pad_head_pairs(w):
    keep = lax.broadcasted_iota(jnp.int32, (w.shape[0], LANE), 1) < MLA_ROPE
    out = []
    for p in range(w.shape[1] // (3 * LANE)):
        nope0, mid, tail = (w[:, (3 * p + g) * LANE:(3 * p + g + 1) * LANE] for g in range(3))
        rolled = pltpu.roll(jnp.concatenate([mid, tail], axis=1), 2 * LANE - MLA_ROPE, 1)
        out += [nope0, jnp.where(keep, mid, 0.0), rolled[:, :LANE], jnp.where(keep, rolled[:, LANE:], 0.0)]
    return jnp.concatenate(out, axis=1)


def _mm_kernel(*refs, rope_scale, w_t, pad_heads):
    if rope_scale is not None:
        a_ref, w_ref, cos_ref, sin_ref, o_ref = refs
    else:
        a_ref, w_ref, o_ref = refs
    w = _pad_head_pairs(w_ref[...]) if pad_heads else w_ref[...]
    a, w = a_ref[...].astype(BF16), w.astype(BF16)
    if w_t:
        acc = lax.dot_general(a, w, (((1,), (1,)), ((), ())), preferred_element_type=F32)
    else:
        acc = jnp.dot(a, w, preferred_element_type=F32)
    if rope_scale is not None:
        cos, sin = cos_ref[...], sin_ref[...]
        for g in range(acc.shape[1] // LANE):
            blk = acc[:, g * LANE:(g + 1) * LANE]
            if g % 2 == 1:
                blk = _rope_lanes(blk, cos, sin)
            o_ref[:, g * LANE:(g + 1) * LANE] = (blk * rope_scale).astype(o_ref.dtype)
    else:
        o_ref[...] = acc.astype(o_ref.dtype)


def _pick_blocks(r, k, n_out, a_itemsize, out_itemsize, tm_cap):
    tm = max(t for t in range(16, tm_cap + 1, 16) if r % t == 0)
    budget = VMEM_BLOCK_FRACTION * VMEM_LIMIT_BYTES
    candidates = [t for t in (2048, 1024, 512, 256) if n_out % t == 0] or [n_out]
    for tn in candidates:
        fixed = k * tn * (2 * 4 + 2) + 2 * tm * tn * out_itemsize
        for a_copies in (2, 1):
            if fixed + a_copies * tm * k * a_itemsize <= budget:
                return tm, tn, a_copies == 1
    raise ValueError("no matmul tiling fits VMEM")


def matmul(a, w, *, layer=None, n_out=None, w_t=False, col0=0, tm_cap=1088, out_dtype=F32, rope=None,
           rope_scale=None, pad_heads=False):
    r, k = a.shape
    n_w, k_w = (w.shape[-2], w.shape[-1]) if w_t else (w.shape[-1], w.shape[-2])
    n_out = n_w if n_out is None else n_out
    tm, tn, single_a = _pick_blocks(r, k, n_out, a.dtype.itemsize, jnp.dtype(out_dtype).itemsize, tm_cap)
    assert n_out % tn == 0 and col0 % tn == 0 and k_w == k
    cb0 = col0 // tn
    if pad_heads:
        assert not w_t and tn % (4 * LANE) == 0 and n_w * 4 == n_out * 3
    w_block = (tn, k) if w_t else (k, tn * 3 // 4 if pad_heads else tn)
    w_index = (lambda j: (cb0 + j, 0)) if w_t else (lambda j: (0, cb0 + j))
    if layer is None:
        w_spec = pl.BlockSpec(w_block, lambda i, j: w_index(j))
    else:
        w_spec = pl.BlockSpec((None,) + w_block, lambda i, j: (layer,) + w_index(j))
    args = [a, w]
    a_mode = {"pipeline_mode": pl.Buffered(1)} if single_a else {}
    in_specs = [pl.BlockSpec((tm, k), lambda i, j: (i, 0), **a_mode), w_spec]
    if rope is not None:
        args += list(rope)
        in_specs += [pl.BlockSpec((tm, LANE), lambda i, j: (i, 0))] * 2
    return pl.pallas_call(
        functools.partial(_mm_kernel, rope_scale=rope_scale if rope is not None else None, w_t=w_t,
                          pad_heads=pad_heads),
        grid=(r // tm, n_out // tn),
        in_specs=in_specs,
        out_specs=pl.BlockSpec((tm, tn), lambda i, j: (i, j)),
        out_shape=jax.ShapeDtypeStruct((r, n_out), out_dtype),
        compiler_params=_params("arbitrary", "arbitrary"),
        name="matmul",
    )(*args)


def _col_to_row(col, eye):
    return jnp.sum(jnp.where(eye, col, 0.0), axis=0, keepdims=True)


def _mlstm_chunk(q, k, v, ig, lf, mask, eye, c_ref, n_ref, m_ref):
    neg_inf = -jnp.inf
    lf_row = _col_to_row(lf, eye)
    b = jnp.sum(jnp.where(mask, lf_row, 0.0), axis=1, keepdims=True)
    b_end = jnp.sum(lf, axis=0, keepdims=True)
    src = ig - b
    dmat = jnp.where(mask, b + _col_to_row(src, eye), neg_inf)
    m_prev = m_ref[...]
    inter = b + m_prev
    m_out = jnp.maximum(inter, jnp.max(dmat, axis=1, keepdims=True))
    w_intra = jnp.exp(dmat - m_out)
    w_inter = jnp.exp(inter - m_out)
    qk = lax.dot_general(q, k, (((1,), (1,)), ((), ())), preferred_element_type=F32)
    s = qk * w_intra
    c_prev = c_ref[...]
    num = jnp.dot(s.astype(BF16), v, preferred_element_type=F32)
    num = num + w_inter * jnp.dot(q, c_prev.astype(BF16), preferred_element_type=F32)
    qn = jnp.sum(q.astype(F32) * n_ref[...], axis=1, keepdims=True)
    den = jnp.sum(s, axis=1, keepdims=True) + w_inter * qn
    den = jnp.maximum(jnp.abs(den), jnp.exp(-m_out))
    h = num / den
    g = b_end + src
    m_new = jnp.maximum(b_end + m_prev, jnp.max(g, axis=0, keepdims=True))
    wg = jnp.exp(g - m_new)
    decay = jnp.exp(b_end + m_prev - m_new)
    kw = k.astype(F32) * wg
    c_ref[...] = decay * c_prev + lax.dot_general(
        kw.astype(BF16), v, (((0,), (0,)), ((), ())), preferred_element_type=F32)
    n_ref[...] = decay * n_ref[...] + jnp.sum(kw, axis=0, keepdims=True)
    m_ref[...] = m_new
    return h


def _mlstm_kernel(qf_ref, kf_ref, vf_ref, gf_ref, qb_ref, kb_ref, vb_ref, gb_ref, bias_ref,
                  hf_ref, hb_ref, cf_ref, nf_ref, mf_ref, cb_ref, nb_ref, mb_ref, *, dk_scale):
    heads_per_step = cf_ref.shape[0]
    dk, dv = cf_ref.shape[1], cf_ref.shape[2]

    @pl.when(pl.program_id(2) == 0)
    def _():
        for ref in (cf_ref, nf_ref, mf_ref, cb_ref, nb_ref, mb_ref):
            ref[...] = jnp.zeros_like(ref)

    chunk = qf_ref.shape[0]
    row = lax.broadcasted_iota(jnp.int32, (chunk, chunk), 0)
    col = lax.broadcasted_iota(jnp.int32, (chunk, chunk), 1)
    eye = row == col
    lane = lax.broadcasted_iota(jnp.int32, gf_ref.shape, 1)

    def gate_cols(g_ref, first, head):
        g = g_ref[...] + bias_ref[...]
        g = GATE_CAP * jnp.tanh(g / GATE_CAP)
        ig = jnp.sum(jnp.where(lane == first * ML_HEADS + head, g, 0.0), axis=1, keepdims=True)
        fg = jnp.sum(jnp.where(lane == (first + 1) * ML_HEADS + head, g, 0.0), axis=1, keepdims=True)
        return ig, jax.nn.log_sigmoid(fg)

    def run(i, q_ref, k_ref, v_ref, g_ref, first, mask, c_ref, n_ref, m_ref, h_ref):
        ig, lf = gate_cols(g_ref, first, pl.program_id(1) * heads_per_step + i)
        qk_cols, v_cols = slice(i * dk, (i + 1) * dk), slice(i * dv, (i + 1) * dv)
        q = q_ref[:, qk_cols] * dk_scale
        h_ref[:, v_cols] = _mlstm_chunk(q, k_ref[:, qk_cols], v_ref[:, v_cols], ig, lf, mask, eye,
                                        c_ref.at[i], n_ref.at[i], m_ref.at[i])

    for i in range(heads_per_step):
        run(i, qf_ref, kf_ref, vf_ref, gf_ref, 0, col <= row, cf_ref, nf_ref, mf_ref, hf_ref)
        run(i, qb_ref, kb_ref, vb_ref, gb_ref, 2, col >= row, cb_ref, nb_ref, mb_ref, hb_ref)


def mlstm_scan(qkv, gates, gate_bias, *, lay, dk, dv):
    r = qkv.shape[0]
    n_gates = gates.shape[1]
    st = lay.seq_tiles

    def fwd_tile(b, c):
        return jnp.where(c == 0, lay.ctx_tile(b), b * st + c - 1)

    def bwd_tile(b, c):
        return jnp.where(c == 0, lay.ctx_tile(b), b * st + st - c)

    hp = SCAN_HEADS_PER_STEP
    assert ML_HEADS % hp == 0

    def specs(tile):
        return [
            pl.BlockSpec((ROW_TILE, hp * dk), lambda b, h, c: (tile(b, c), h)),
            pl.BlockSpec((ROW_TILE, hp * dk), lambda b, h, c: (tile(b, c), ML_HEADS // hp + h)),
            pl.BlockSpec((ROW_TILE, hp * dv), lambda b, h, c: (tile(b, c), (2 * ML_HEADS * dk) // (hp * dv) + h)),
            pl.BlockSpec((ROW_TILE, n_gates), lambda b, h, c: (tile(b, c), 0)),
        ]

    def out_spec(tile):
        return pl.BlockSpec((ROW_TILE, hp * dv), lambda b, h, c: (tile(b, c), h))

    state = [pltpu.VMEM((hp, dk, dv), F32), pltpu.VMEM((hp, 1, dk), F32), pltpu.VMEM((hp, 1, 1), F32)]
    return pl.pallas_call(
        functools.partial(_mlstm_kernel, dk_scale=dk ** -0.5),
        grid=(lay.batch, ML_HEADS // hp, st + 1),
        in_specs=specs(fwd_tile) + specs(bwd_tile) + [pl.BlockSpec((1, n_gates), lambda b, h, c: (0, 0))],
        out_specs=[out_spec(fwd_tile), out_spec(bwd_tile)],
        out_shape=[jax.ShapeDtypeStruct((r, ML_HEADS * dv), F32)] * 2,
        scratch_shapes=state + state,
        compiler_params=_params("arbitrary", "arbitrary", "arbitrary"),
        name="mlstm_scan",
    )(qkv, qkv, qkv, gates, qkv, qkv, qkv, gates, gate_bias)


def _readout_kernel(hf_ref, hb_ref, o_ref, g_ref, out_ref, *, dv):
    for hd in range(ML_HEADS):
        sl = slice(hd * dv, (hd + 1) * dv)
        hs = hf_ref[:, sl] + hb_ref[:, sl]
        o = o_ref[:, sl]
        out_ref[:, sl] = (jax.nn.sigmoid(o) * _rms(hs, g_ref[:, sl])).astype(out_ref.dtype)


def mlstm_readout(h_f, h_b, o, g_norm, *, dv):
    r, width = h_f.shape
    row = pl.BlockSpec((ROW_TILE, width), lambda t: (t, 0))
    return pl.pallas_call(
        functools.partial(_readout_kernel, dv=dv),
        grid=(r // ROW_TILE,),
        in_specs=[row, row, row, pl.BlockSpec((1, width), lambda t: (0, 0))],
        out_specs=row,
        out_shape=jax.ShapeDtypeStruct((r, width), BF16),
        compiler_params=_params("arbitrary"),
        name="mlstm_readout",
    )(h_f, h_b, o, g_norm)


def _mla_prep_kernel(down_ref, krin_ref, qn_ref, kvn_ref, cos_ref, sin_ref, cq_ref, ckv_ref, kr_ref, *,
                     q_lora, kv_lora):
    cq_ref[...] = _rms(down_ref[:, :q_lora], qn_ref[...]).astype(cq_ref.dtype)
    ckv_ref[...] = _rms(down_ref[:, q_lora:q_lora + kv_lora], kvn_ref[...]).astype(ckv_ref.dtype)
    kr = krin_ref[...]
    lane = lax.broadcasted_iota(jnp.int32, kr.shape, 1)
    kr = jnp.where(lane < MLA_ROPE, kr, 0.0)
    kr_ref[...] = _rope_lanes(kr, cos_ref[...], sin_ref[...]).astype(kr_ref.dtype)


def mla_prep(down, kr_raw, q_norm, kv_norm, cos, sin, *, q_lora, kv_lora):
    r, width = down.shape
    return pl.pallas_call(
        functools.partial(_mla_prep_kernel, q_lora=q_lora, kv_lora=kv_lora),
        grid=(r // ROW_TILE,),
        in_specs=[pl.BlockSpec((ROW_TILE, width), lambda t: (t, 0)),
                  pl.BlockSpec((ROW_TILE, LANE), lambda t: (t, 0)),
                  pl.BlockSpec((1, q_lora), lambda t: (0, 0)),
                  pl.BlockSpec((1, kv_lora), lambda t: (0, 0)),
                  pl.BlockSpec((ROW_TILE, LANE), lambda t: (t, 0)),
                  pl.BlockSpec((ROW_TILE, LANE), lambda t: (t, 0))],
        out_specs=[pl.BlockSpec((ROW_TILE, q_lora), lambda t: (t, 0)),
                   pl.BlockSpec((ROW_TILE, kv_lora), lambda t: (t, 0)),
                   pl.BlockSpec((ROW_TILE, LANE), lambda t: (t, 0))],
        out_shape=[jax.ShapeDtypeStruct((r, q_lora), BF16),
                   jax.ShapeDtypeStruct((r, kv_lora), BF16),
                   jax.ShapeDtypeStruct((r, LANE), BF16)],
        compiler_params=_params("arbitrary"),
        name="mla_prep",
    )(down, kr_raw, q_norm, kv_norm, cos, sin)


def _softmax_attend(q, k_t, v):
    s = jnp.dot(q, k_t, preferred_element_type=F32)
    p = jnp.exp2(s - jnp.max(s, axis=1, keepdims=True))
    o = jnp.dot(p.astype(BF16), v, preferred_element_type=F32)
    return o / jnp.sum(p, axis=1, keepdims=True)


def _attn_lat_kernel(q_ref, knl_ref, krl_ref, vl_ref, knc_ref, krc_ref, vc_ref, o_ref, kcat_ref, vcat_ref):
    ctx_len = vc_ref.shape[0]

    @pl.when(pl.program_id(2) == 0)
    def _():
        kcat_ref[:MLA_NOPE, :ctx_len] = knc_ref[...]
        kcat_ref[MLA_NOPE:, :ctx_len] = krc_ref[...]
        kcat_ref[:MLA_NOPE, ctx_len:] = knl_ref[...]
        kcat_ref[MLA_NOPE:, ctx_len:] = krl_ref[...]
        vcat_ref[:ctx_len, :] = vc_ref[...]
        vcat_ref[ctx_len:, :] = vl_ref[...]

    k_t, v = kcat_ref[...], vcat_ref[...]
    sub = min(ATTN_SUB_TILE, q_ref.shape[0])
    for part in range(q_ref.shape[0] // sub):
        rows = slice(part * sub, (part + 1) * sub)
        o_ref[rows, :] = _softmax_attend(q_ref[rows, :], k_t, v).astype(o_ref.dtype)


def _attn_ctx_kernel(q_ref, kn_ref, kr_ref, v_ref, prev_ref, o_ref):
    del prev_ref
    k_t = jnp.concatenate([kn_ref[...], kr_ref[...]], axis=0)
    o_ref[...] = _softmax_attend(q_ref[...], k_t, v_ref[...]).astype(o_ref.dtype)


def mla_attention(q, k_t, kr_t, v, *, lay, heads, with_ctx):
    seq = lay.seq_tiles * ROW_TILE
    tq = min(ATTN_Q_TILE, seq)
    q_tiles = seq // tq
    out_rows = lay.rows if with_ctx else lay.lat_rows

    def keys_lat(row):
        return pl.BlockSpec((MLA_NOPE, seq), lambda b, h, j: (row(h), b))

    def keys_ctx(row):
        return pl.BlockSpec((MLA_NOPE, ROW_TILE), lambda b, h, j: (row(h), lay.ctx_tile(b)))

    own, shared = (lambda h: h), (lambda h: 0)
    att = pl.pallas_call(
        _attn_lat_kernel,
        grid=(lay.batch, heads, q_tiles),
        in_specs=[pl.BlockSpec((tq, MLA_QW), lambda b, h, j: (b * q_tiles + j, h)),
                  keys_lat(own), keys_lat(shared), pl.BlockSpec((seq, MLA_V), lambda b, h, j: (b, h)),
                  keys_ctx(own), keys_ctx(shared),
                  pl.BlockSpec((ROW_TILE, MLA_V), lambda b, h, j: (lay.ctx_tile(b), h))],
        out_specs=pl.BlockSpec((tq, MLA_V), lambda b, h, j: (b * q_tiles + j, h)),
        out_shape=jax.ShapeDtypeStruct((out_rows, heads * MLA_V), BF16),
        scratch_shapes=[pltpu.VMEM((MLA_QW, ROW_TILE + seq), BF16), pltpu.VMEM((ROW_TILE + seq, MLA_V), BF16)],
        compiler_params=_params("arbitrary", "arbitrary", "arbitrary"),
        name="mla_attention",
    )(q, k_t, kr_t, v, k_t, kr_t, v)
    if not with_ctx:
        return att

    def ctx_keys(row):
        return pl.BlockSpec((MLA_NOPE, ROW_TILE), lambda b, h: (row(h), lay.ctx_tile(b)))

    def ctx_rows(width):
        return pl.BlockSpec((ROW_TILE, width), lambda b, h: (lay.ctx_tile(b), h))

    return pl.pallas_call(
        _attn_ctx_kernel,
        grid=(lay.batch, heads),
        in_specs=[ctx_rows(MLA_QW), ctx_keys(own), ctx_keys(shared), ctx_rows(MLA_V),
                  pl.BlockSpec(memory_space=pl.ANY)],
        out_specs=ctx_rows(MLA_V),
        out_shape=jax.ShapeDtypeStruct(att.shape, att.dtype),
        input_output_aliases={4: 0},
        compiler_params=_params("arbitrary", "arbitrary"),
        name="mla_attention_ctx",
    )(q, k_t, kr_t, v, att)


def _conv_gate_kernel(ug_ref, uv_ref, pg_ref, pv_ref, ng_ref, nv_ref, wg_ref, wv_ref, bg_ref, bv_ref, o_ref, *, lay):
    t = pl.program_id(0)
    first, last = lay.seq_first(t), lay.seq_last(t)
    rows = ug_ref.shape[0]
    ridx = lax.broadcasted_iota(jnp.int32, ug_ref.shape, 0)

    def conv(u_ref, p_ref, n_ref, w_ref, b_ref):
        u = u_ref[...]
        prev_row = jnp.where(first, 0.0, p_ref[SUBLANE - 1:SUBLANE, :])
        next_row = jnp.where(last, 0.0, n_ref[0:1, :])
        up = jnp.where(ridx == 0, prev_row, pltpu.roll(u, 1, 0))
        un = jnp.where(ridx == rows - 1, next_row, pltpu.roll(u, rows - 1, 0))
        return up * w_ref[0:1, :] + u * w_ref[1:2, :] + un * w_ref[2:3, :] + b_ref[...]

    gate = conv(ug_ref, pg_ref, ng_ref, wg_ref, bg_ref)
    val = conv(uv_ref, pv_ref, nv_ref, wv_ref, bv_ref)
    o_ref[...] = (gate * jax.nn.sigmoid(gate) * val).astype(o_ref.dtype)


def conv_gate(u, conv_w, conv_b, *, lay):
    r, two_f = u.shape
    f = two_f // 2
    tn = next(t for t in (CONV_COL_TILE, CONV_COL_TILE // 2, CONV_COL_TILE // 4, LANE) if f % t == 0)
    nb = f // tn
    per_tile = ROW_TILE // SUBLANE
    last_sub = r // SUBLANE - 1

    def main(off):
        return pl.BlockSpec((ROW_TILE, tn), lambda t, j: (t, off + j))

    def prev(off):
        return pl.BlockSpec((SUBLANE, tn), lambda t, j: (jnp.maximum(t * per_tile - 1, 0), off + j))

    def nxt(off):
        return pl.BlockSpec((SUBLANE, tn), lambda t, j: (jnp.minimum((t + 1) * per_tile, last_sub), off + j))

    def wspec(off):
        return pl.BlockSpec((3, tn), lambda t, j: (0, off + j))

    def bspec(off):
        return pl.BlockSpec((1, tn), lambda t, j: (0, off + j))

    return pl.pallas_call(
        functools.partial(_conv_gate_kernel, lay=lay),
        grid=(r // ROW_TILE, nb),
        in_specs=[main(0), main(nb), prev(0), prev(nb), nxt(0), nxt(nb), wspec(0), wspec(nb), bspec(0), bspec(nb)],
        out_specs=pl.BlockSpec((ROW_TILE, tn), lambda t, j: (t, j)),
        out_shape=jax.ShapeDtypeStruct((r, f), BF16),
        compiler_params=_params("arbitrary", "arbitrary"),
        name="conv_gate",
    )(u, u, u, u, u, u, conv_w, conv_w, conv_b.reshape(1, two_f), conv_b.reshape(1, two_f))


def _rope_tables(batch, ctx_len, seq):
    per_axis = MLA_ROPE // 2
    inv_freq = np.float32(ROPE_THETA) ** (-np.arange(0, per_axis, 2, dtype=np.float32) / np.float32(per_axis))
    pos = np.arange(seq)
    ang_r = (pos // GRID_W).astype(np.float32)[:, None] * inv_freq
    ang_c = (pos % GRID_W).astype(np.float32)[:, None] * inv_freq
    cos64 = np.concatenate([np.cos(ang_r)] * 2 + [np.cos(ang_c)] * 2, axis=1)
    sin64 = np.concatenate([-np.sin(ang_r), np.sin(ang_r), -np.sin(ang_c), np.sin(ang_c)], axis=1)
    cos_lat = np.tile(np.concatenate([cos64, cos64], axis=1), (batch, 1))
    sin_lat = np.tile(np.concatenate([sin64, sin64], axis=1), (batch, 1))
    cos = np.concatenate([cos_lat, np.ones((batch * ctx_len, LANE), np.float32)], axis=0)
    sin = np.concatenate([sin_lat, np.zeros((batch * ctx_len, LANE), np.float32)], axis=0)
    return jnp.asarray(cos, F32), jnp.asarray(sin, F32)


def kernel(x, c, ctx, c_ctx, mod_w, mod_b, norm_w, ml_w_in, ml_b_gate, ml_norm, ml_w_out, mla_w_down, mla_q_norm,
           mla_w_uq, mla_kv_norm, mla_w_ukv, mla_w_out, ffn_w_up, ffn_conv_w, ffn_conv_b, ffn_w_down):
    batch, seq, d = x.shape
    ctx_len = ctx.shape[1]
    depth = mod_w.shape[0]
    lay = Layout(batch, seq, ctx_len)
    ml_qk = d // 2
    dk, dv = ml_qk // ML_HEADS, d // ML_HEADS
    heads = d // MLA_NOPE
    q_lora, kv_lora = d // 4, d // 8

    h = (x.reshape(batch * seq, d), ctx.reshape(batch * ctx_len, d))
    cond = jnp.concatenate([c_ctx[None], c, jnp.zeros((MOD_ROWS - 1 - batch, d), F32)], axis=0)
    mods = adaln_all(cond, mod_w, mod_b).reshape(depth * MOD_ROWS * N_MOD, 1, d)
    gains = norm_w.reshape(depth * 4, 1, d)
    cos, sin = _rope_tables(batch, ctx_len, seq)
    norm = functools.partial(norm_modulate, mods=mods, norm_w=gains, lay=lay)
    w_in_t = jnp.swapaxes(ml_w_in, 1, 2)
    w_down_t = jnp.swapaxes(mla_w_down, 1, 2)

    f = None
    for i in range(depth):
        j = i // 2
        ctx_out = i < depth - 1
        if f is None:
            a = norm(h, layer=i, rows=lay.rows, pre_idx=0, shift_idx=0, scale_idx=1)[0]
        else:
            h, a = norm(h, layer=i, rows=lay.rows, y=f, y_layer=i - 1, gate_idx=5, post_idx=3,
                        pre_idx=0, shift_idx=0, scale_idx=1)
        if i % 2 == 0:
            qkv = matmul(a, w_in_t, layer=j, w_t=True, n_out=2 * ml_qk + d, out_dtype=BF16)
            o = matmul(a, w_in_t, layer=j, w_t=True, n_out=d, col0=2 * ml_qk + d)
            gates = matmul(a, w_in_t, layer=j, w_t=True, n_out=4 * ML_HEADS, col0=2 * ml_qk + 2 * d)
            bias = ml_b_gate[j].reshape(1, 4 * ML_HEADS)
            h_f, h_b = mlstm_scan(qkv, gates, bias, lay=lay, dk=dk, dv=dv)
            mixed = mlstm_readout(h_f, h_b, o, ml_norm[j].reshape(1, d), dv=dv)
        else:
            down = matmul(a, w_down_t, layer=j, w_t=True, n_out=q_lora + kv_lora)
            kr_raw = matmul(a, w_down_t, layer=j, w_t=True, n_out=LANE, col0=q_lora + kv_lora)
            cq, ckv, kr = mla_prep(down, kr_raw, mla_q_norm[j].reshape(1, q_lora),
                                   mla_kv_norm[j].reshape(1, kv_lora), cos, sin, q_lora=q_lora, kv_lora=kv_lora)
            q = matmul(cq, mla_w_uq, layer=j, pad_heads=True, n_out=heads * MLA_QW, out_dtype=BF16,
                       rope=(cos, sin), rope_scale=MLA_SCALE * math.log2(math.e))
            w_kv = mla_w_ukv[j].reshape(kv_lora, heads, MLA_NOPE + MLA_V)
            w_k_t = jnp.transpose(w_kv[:, :, :MLA_NOPE].reshape(kv_lora, heads * MLA_NOPE))
            w_v = w_kv[:, :, MLA_NOPE:].reshape(kv_lora, heads * MLA_V)
            k_t = matmul(w_k_t, ckv, w_t=True, out_dtype=BF16)
            v = matmul(ckv, w_v, out_dtype=BF16)
            mixed = mla_attention(q, k_t, jnp.transpose(kr), v, lay=lay, heads=heads, with_ctx=ctx_out)
        rows = mixed.shape[0]
        y = matmul(mixed, ml_w_out if i % 2 == 0 else mla_w_out, layer=j)
        h, a = norm(h, layer=i, rows=rows, y=y, gate_idx=2, post_idx=1, pre_idx=2, shift_idx=3, scale_idx=4)
        u = matmul(a, ffn_w_up, layer=i)
        g = conv_gate(u, ffn_conv_w[i], ffn_conv_b[i], lay=lay)
        f = matmul(g, ffn_w_down, layer=i)
    out = norm(h, layer=depth - 1, rows=lay.lat_rows, y=f, gate_idx=5, post_idx=3)[0]
    return out.reshape(batch, seq, d)
```
